```python
import math
import jax, jax.numpy as jnp
from jax import lax
import numpy as np

D_MODEL = 1024
BATCH = 8
SEQ = 8192
DEPTH = 2

N_MIXERS = 2
MEM_LEN = 256
HEAD_DIM = 64
MIX_WIDTH = D_MODEL
MEM_HEADS = 4
MEM_WIDTH = MEM_HEADS * HEAD_DIM
TOK_WIDTH = MIX_WIDTH - MEM_WIDTH
LRU_BLOCKS = TOK_WIDTH // HEAD_DIM
LRU_BLOCK = TOK_WIDTH // LRU_BLOCKS
CONV_W = 4
LRU_C = 8.0
ATTN_HEADS = TOK_WIDTH // HEAD_DIM
KV_LATENT = 128
IDX_HEADS = 4
IDX_DIM = 64
TOPK_MAX = 256
Q_BLOCK = 128
REL_BUCKETS = 32
REL_MAX_DIST = 128
N_EXPERTS = 32
TOP_K = 4
D_EXPERT = D_MODEL
SWIGLU_LIMIT = 7.0
SWIGLU_ALPHA = 1.702
EXPERT_BLOCK = 128
DN_ALPHA = (2 * DEPTH) ** 0.25
DN_BETA = (8 * DEPTH) ** -0.25
N_A = (DEPTH + 1) // 2
N_B = DEPTH // 2
W_IN_A = 2 * TOK_WIDTH + MEM_WIDTH
B_SPLITS = [TOK_WIDTH, TOK_WIDTH + KV_LATENT, TOK_WIDTH + KV_LATENT + IDX_HEADS * IDX_DIM,
            TOK_WIDTH + KV_LATENT + IDX_HEADS * IDX_DIM + IDX_DIM,
            TOK_WIDTH + KV_LATENT + IDX_HEADS * IDX_DIM + IDX_DIM + IDX_HEADS]
W_IN_B = B_SPLITS[-1] + MEM_WIDTH

kernel_name = "hybrid_rglru_dsa_moe_block"


def layer_norm(x, g, b, eps=1e-5):
    xf = x.astype(jnp.float32)
    mu = jnp.mean(xf, axis=-1, keepdims=True)
    xc = xf - mu
    var = jnp.mean(xc * xc, axis=-1, keepdims=True)
    return (xc * lax.rsqrt(var + eps) * g + b).astype(x.dtype)


def rms_norm(x, g, eps=1e-6):
    xf = x.astype(jnp.float32)
    return (xf * lax.rsqrt(jnp.mean(xf * xf, axis=-1, keepdims=True) + eps) * g).astype(x.dtype)


def t5_bucket(rel):
    n = jnp.maximum(rel, 0)
    max_exact = REL_BUCKETS // 2
    large = max_exact + (jnp.log(jnp.maximum(n, 1).astype(jnp.float32) / max_exact)
                         / math.log(REL_MAX_DIST / max_exact) * (REL_BUCKETS - max_exact)).astype(jnp.int32)
    large = jnp.minimum(large, REL_BUCKETS - 1)
    return jnp.where(n < max_exact, n, large)


def causal_dwconv(x, w, b):
    C = x.shape[-1]
    y = lax.conv_general_dilated(x, w[:, None, :], window_strides=(1,), padding=[(CONV_W - 1, 0)],
                                 dimension_numbers=('NWC', 'WIO', 'NWC'), feature_group_count=C)
    return y + b


def rg_lru(xc, w_r, b_r, w_i, b_i, lam):
    B_, S_, C = xc.shape
    xb = xc.reshape(B_, S_, LRU_BLOCKS, LRU_BLOCK)
    r = jax.nn.sigmoid(jnp.einsum('bsnc,ncd->bsnd', xb, w_r).reshape(B_, S_, C) + b_r)
    i = jax.nn.sigmoid(jnp.einsum('bsnc,ncd->bsnd', xb, w_i).reshape(B_, S_, C) + b_i)
    log_a = -LRU_C * r.astype(jnp.float32) * jax.nn.softplus(-lam.astype(jnp.float32))
    a = jnp.exp(log_a)
    u = jnp.sqrt(-jnp.expm1(2.0 * log_a)) * (i * xc).astype(jnp.float32)

    def combine(e1, e2):
        return (e1[0] * e2[0], e2[0] * e1[1] + e2[1])

    _, h = lax.associative_scan(combine, (a, u), axis=1)
    return h.astype(xc.dtype)


def memory_cross_attention(q, mem, w_mem_kv):
    B_, S_ = q.shape[:2]
    k, v = jnp.split(mem @ w_mem_kv, 2, axis=-1)
    qh = q.reshape(B_, S_, MEM_HEADS, HEAD_DIM)
    kh = k.reshape(B_, -1, MEM_HEADS, HEAD_DIM)
    vh = v.reshape(B_, -1, MEM_HEADS, HEAD_DIM)
    logits = jnp.einsum('bshd,bmhd->bhsm', qh, kh).astype(jnp.float32) * HEAD_DIM ** -0.5
    p = jax.nn.softmax(logits, axis=-1).astype(vh.dtype)
    return jnp.einsum('bhsm,bmhd->bshd', p, vh).reshape(B_, S_, MEM_WIDTH)


def dsa_attention(q, c_kv, iq, ik, iw, w_uk, w_uv, rel_bias):
    B_, S_ = c_kv.shape[:2]
    k_sel = min(TOPK_MAX, S_ // 4)
    nqb = S_ // Q_BLOCK
    q_lat = jnp.einsum('bshd,rhd->bshr', q, w_uk) * HEAD_DIM ** -0.5
    iw = iw.astype(jnp.float32) * (IDX_HEADS ** -0.5 * IDX_DIM ** -0.5)
    key_pos = jnp.arange(S_, dtype=jnp.int32)

    def to_blocks(t):
        return jnp.moveaxis(t.reshape(B_, nqb, Q_BLOCK, *t.shape[2:]), 1, 0)

    def block(args):
        ql, iqb, iwb, start = args
        qpos = start + jnp.arange(Q_BLOCK, dtype=jnp.int32)
        s = jnp.einsum('bqhd,bsd->bqhs', iqb, ik).astype(jnp.float32)
        score = jnp.einsum('bqhs,bqh->bqs', jax.nn.relu(s), iwb)
        score = jnp.where(key_pos[None, None, :] <= qpos[None, :, None], score, -jnp.inf)
        _, idx = lax.top_k(score, k_sel)
        c_sel = jax.vmap(lambda cb, ib: cb[ib])(c_kv, idx)
        rel = qpos[None, :, None] - idx
        bias = jnp.moveaxis(rel_bias[t5_bucket(rel)], -1, -2)
        logits = jnp.einsum('bqhr,bqkr->bqhk', ql, c_sel).astype(jnp.float32) + bias.astype(jnp.float32)
        logits = jnp.where((rel >= 0)[:, :, None, :], logits, -jnp.inf)
        p = jax.nn.softmax(logits, axis=-1).astype(c_sel.dtype)
        o_lat = jnp.einsum('bqhk,bqkr->bqhr', p, c_sel)
        return jnp.einsum('bqhr,rhd->bqhd', o_lat, w_uv)

    starts = jnp.arange(nqb, dtype=jnp.int32) * Q_BLOCK
    out = lax.map(block, (to_blocks(q_lat), to_blocks(iq), to_blocks(iw), starts))
    return jnp.moveaxis(out, 0, 1).reshape(B_, S_, ATTN_HEADS * HEAD_DIM)


def moe(x, router_w, router_b, w1, b1, w2, b2):
    B_, S_, D = x.shape
    xt = x.reshape(-1, D)
    T = xt.shape[0]
    logits = (xt @ router_w + router_b).astype(jnp.float32)
    top_v, top_e = lax.top_k(logits, TOP_K)
    gate = jax.nn.softmax(top_v, axis=-1)
    A = T * TOP_K
    e_flat = top_e.reshape(-1)
    g_flat = gate.reshape(-1).astype(x.dtype)
    tok_flat = jnp.arange(A, dtype=jnp.int32) // TOP_K
    order = jnp.argsort(e_flat)
    se = e_flat[order]
    counts = jnp.bincount(e_flat, length=N_EXPERTS)
    padded = (counts + EXPERT_BLOCK - 1) // EXPERT_BLOCK * EXPERT_BLOCK
    pend = jnp.cumsum(padded)
    pstart = pend - padded
    ustart = jnp.cumsum(counts) - counts
    dest = pstart[se] + jnp.arange(A, dtype=jnp.int32) - ustart[se]
    nblk = -(-A // EXPERT_BLOCK) + N_EXPERTS
    P = nblk * EXPERT_BLOCK
    row_tok = jnp.zeros((P,), jnp.int32).at[dest].set(tok_flat[order])
    row_g = jnp.zeros((P,), x.dtype).at[dest].set(g_flat[order])
    blk_e = jnp.minimum(jnp.searchsorted(pend, jnp.arange(nblk, dtype=jnp.int32) * EXPERT_BLOCK, side='right'),
                        N_EXPERTS - 1)

    def expert_block(args):
        tok, g, e = args
        h = xt[tok] @ w1[e] + b1[e]
        gt = jnp.minimum(h[:, :D_EXPERT], SWIGLU_LIMIT)
        up = jnp.clip(h[:, D_EXPERT:], -SWIGLU_LIMIT, SWIGLU_LIMIT)
        act = (up + 1.0) * (gt * jax.nn.sigmoid(SWIGLU_ALPHA * gt))
        return (act @ w2[e] + b2[e]) * g[:, None]

    y = lax.map(expert_block, (row_tok.reshape(nblk, EXPERT_BLOCK), row_g.reshape(nblk, EXPERT_BLOCK), blk_e))
    y = jax.ops.segment_sum(y.reshape(P, D), row_tok, num_segments=T)
    return y.reshape(B_, S_, D)


def setup_inputs(seed: int = 0) -> dict:
    key = jax.random.key(seed)
    ks = jax.random.split(key, 32)
    f32 = jnp.float32

    def nrm(k, shape, scale):
        return jax.random.normal(k, shape, f32) * scale

    u = jax.random.uniform(ks[9], (N_A, TOK_WIDTH), f32, 0.9, 0.999)
    p = u ** (1.0 / LRU_C)
    a_lambda = jnp.log(p) - jnp.log1p(-p)
    mem_kv_scale = jnp.concatenate([jnp.ones((MEM_WIDTH,), f32), jnp.full((MEM_WIDTH,), DN_BETA, f32)])
    return {
        'x': nrm(ks[0], (BATCH, SEQ, D_MODEL), 1.0),
        'mem': nrm(ks[1], (BATCH, MEM_LEN, D_MODEL), 1.0),
        'rel_bias': nrm(ks[2], (REL_BUCKETS, ATTN_HEADS), 0.5),
        'a_w_in': nrm(ks[3], (N_A, D_MODEL, W_IN_A), D_MODEL ** -0.5),
        'a_conv_w': nrm(ks[4], (N_A, CONV_W, TOK_WIDTH), CONV_W ** -0.5),
        'a_conv_b': nrm(ks[5], (N_A, TOK_WIDTH), 0.01),
        'a_wr': nrm(ks[6], (N_A, LRU_BLOCKS, LRU_BLOCK, LRU_BLOCK), LRU_BLOCK ** -0.5),
        'a_br': nrm(ks[7], (N_A, TOK_WIDTH), 0.01),
        'a_wi': nrm(ks[8], (N_A, LRU_BLOCKS, LRU_BLOCK, LRU_BLOCK), LRU_BLOCK ** -0.5),
        'a_bi': nrm(ks[10], (N_A, TOK_WIDTH), 0.01),
        'a_lambda': a_lambda,
        'b_w_in': nrm(ks[11], (N_B, D_MODEL, W_IN_B), D_MODEL ** -0.5),
        'b_kv_norm_g': 1.0 + nrm(ks[12], (N_B, KV_LATENT), 0.01),
        'b_w_uk': nrm(ks[13], (N_B, KV_LATENT, ATTN_HEADS, HEAD_DIM), KV_LATENT ** -0.5),
        'b_w_uv': nrm(ks[14], (N_B, KV_LATENT, ATTN_HEADS, HEAD_DIM), KV_LATENT ** -0.5 * DN_BETA),
        'b_idx_norm_g': 1.0 + nrm(ks[15], (N_B, IDX_DIM), 0.01),
        'b_idx_norm_b': nrm(ks[16], (N_B, IDX_DIM), 0.01),
        'w_mem_kv': nrm(ks[17], (DEPTH, D_MODEL, 2 * MEM_WIDTH), D_MODEL ** -0.5) * mem_kv_scale,
        'w_out': nrm(ks[18], (DEPTH, MIX_WIDTH, D_MODEL), MIX_WIDTH ** -0.5 * DN_BETA),
        'ln1_g': 1.0 + nrm(ks[19], (DEPTH, D_MODEL), 0.01),
        'ln1_b': nrm(ks[20], (DEPTH, D_MODEL), 0.01),
        'router_w': nrm(ks[21], (DEPTH, D_MODEL, N_EXPERTS), D_MODEL ** -0.5),
        'router_b': nrm(ks[22], (DEPTH, N_EXPERTS), 0.01),
        'exp_w1': nrm(ks[23], (DEPTH, N_EXPERTS, D_MODEL, 2 * D_EXPERT), D_MODEL ** -0.5),
        'exp_b1': nrm(ks[24], (DEPTH, N_EXPERTS, 2 * D_EXPERT), 0.01),
        'exp_w2': nrm(ks[25], (DEPTH, N_EXPERTS, D_EXPERT, D_MODEL), D_EXPERT ** -0.5 * DN_BETA),
        'exp_b2': nrm(ks[26], (DEPTH, N_EXPERTS, D_MODEL), 0.01),
        'ln2_g': 1.0 + nrm(ks[27], (DEPTH, D_MODEL), 0.01),
        'ln2_b': nrm(ks[28], (DEPTH, D_MODEL), 0.01),
    }


def reference(x, mem, rel_bias, a_w_in, a_conv_w, a_conv_b, a_wr, a_br, a_wi, a_bi, a_lambda,
              b_w_in, b_kv_norm_g, b_w_uk, b_w_uv, b_idx_norm_g, b_idx_norm_b,
              w_mem_kv, w_out, ln1_g, ln1_b, router_w, router_b, exp_w1, exp_b1, exp_w2, exp_b2,
              ln2_g, ln2_b):
    B_, S_, _ = x.shape
    for layer in range(DEPTH):
        j = layer // N_MIXERS
        if layer % N_MIXERS == 0:
            proj = x @ a_w_in[j]
            xb, gb, mq = jnp.split(proj, [TOK_WIDTH, 2 * TOK_WIDTH], axis=-1)
            xc = causal_dwconv(xb, a_conv_w[j], a_conv_b[j])
            h = rg_lru(xc, a_wr[j], a_br[j], a_wi[j], a_bi[j], a_lambda[j])
            tok = h * jax.nn.gelu(gb)
        else:
            proj = x @ b_w_in[j]
            q, c, iq, ik, iw, mq = jnp.split(proj, B_SPLITS, axis=-1)
            q = q.reshape(B_, S_, ATTN_HEADS, HEAD_DIM)
            c = rms_norm(c, b_kv_norm_g[j])
            iq = iq.reshape(B_, S_, IDX_HEADS, IDX_DIM)
            ik = layer_norm(ik, b_idx_norm_g[j], b_idx_norm_b[j])
            tok = dsa_attention(q, c, iq, ik, iw, b_w_uk[j], b_w_uv[j], rel_bias)
        mix = jnp.concatenate([tok, memory_cross_attention(mq, mem, w_mem_kv[layer])], axis=-1)
        x = layer_norm(DN_ALPHA * x + mix @ w_out[layer], ln1_g[layer], ln1_b[layer])
        ffn = moe(x, router_w[layer], router_b[layer], exp_w1[layer], exp_b1[layer], exp_w2[layer], exp_b2[layer])
        x = layer_norm(DN_ALPHA * x + ffn, ln2_g[layer], ln2_b[layer])
    return x
```

```python
import functools
import math

import jax
import jax.numpy as jnp
import numpy as np
from jax import lax
from jax.experimental import pallas as pl
from jax.experimental.pallas import tpu as pltpu

F32 = jnp.float32
BF16 = jnp.bfloat16
I32 = jnp.int32

HEAD_DIM = 64
MEM_HEADS = 4
MEM_WIDTH = MEM_HEADS * HEAD_DIM
LRU_C = 8.0
CONV_W = 4
KV_LATENT = 128
IDX_HEADS = 4
IDX_DIM = 64
TOPK_MAX = 256
REL_BUCKETS = 32
REL_MAX_DIST = 128
N_EXPERTS = 32
TOP_K = 4
SWIGLU_LIMIT = 7.0
SWIGLU_ALPHA = 1.702
DEPTH = 2
DN_ALPHA = (2 * DEPTH) ** 0.25
LN_EPS = 1e-5
RMS_EPS = 1e-6

SUBLANES = 8
LANES = 128
MASK_BITS = 32
INT_MIN = -2 ** 31
NEG_BIG = -1e30
VMEM_LIMIT = 56 * 1024 * 1024


def _tiles(seq):
    chunk = seq // MASK_BITS
    return dict(
        ts=min(512, seq),
        tq_sel=128,
        chunk=chunk,
        tq_att=chunk,
        tt=256,
        tm=256,
        tr=512,
    )


def _dot(a, b):
    return jnp.dot(a, b, preferred_element_type=F32)


def _dot_nt(a, b):
    return lax.dot_general(a, b, (((1,), (1,)), ((), ())), preferred_element_type=F32)


def _sigmoid(x):
    return 1.0 / (1.0 + jnp.exp(-x))


def _layer_norm(z, g, b):
    mu = jnp.mean(z, axis=-1, keepdims=True)
    zc = z - mu
    var = jnp.mean(zc * zc, axis=-1, keepdims=True)
    return zc * lax.rsqrt(var + LN_EPS) * g + b


def _params(*sem):
    return pltpu.CompilerParams(dimension_semantics=sem, vmem_limit_bytes=VMEM_LIMIT)


def _mm_kernel(a_ref, b_ref, o_ref):
    o_ref[...] = _dot(a_ref[...].astype(BF16), b_ref[...])


def _matmul(a, b, tm):
    m, k = a.shape
    n = b.shape[1]
    return pl.pallas_call(
        _mm_kernel,
        grid=(m // tm,),
        in_specs=[pl.BlockSpec((tm, k), lambda i: (i, 0)),
                  pl.BlockSpec((k, n), lambda i: (0, 0))],
        out_specs=pl.BlockSpec((tm, n), lambda i: (i, 0)),
        out_shape=jax.ShapeDtypeStruct((m, n), F32),
        compiler_params=_params("arbitrary"),
        name="matmul",
    )(a, b)


def _rglru_kernel(x_ref, win_ref, cw_ref, cb_ref, wr_ref, br_ref, wi_ref, bi_ref, lam_ref,
                  tok_ref, mq_ref, tail_ref, h_ref, a_s, u_s, *, tok_w):
    s = pl.program_id(1)

    @pl.when(s == 0)
    def _():
        tail_ref[...] = jnp.zeros_like(tail_ref)
        h_ref[...] = jnp.zeros_like(h_ref)

    proj = _dot(x_ref[0].astype(BF16), win_ref[...])
    xb = proj[:, :tok_w]
    gb = proj[:, tok_w:2 * tok_w]
    mq_ref[0] = proj[:, 2 * tok_w:]
    ts = xb.shape[0]

    xe = jnp.concatenate([tail_ref[...], xb], axis=0)
    tail_ref[...] = xb[ts - SUBLANES:, :]
    xc = xb * cw_ref[CONV_W - 1:CONV_W, :] + cb_ref[...]
    for d in range(1, CONV_W):
        sh = pltpu.roll(xe, d, 0)[SUBLANES:, :]
        xc = xc + sh * cw_ref[CONV_W - 1 - d:CONV_W - d, :]

    xcb = xc.astype(BF16)
    r = _sigmoid(_dot(xcb, wr_ref[...]) + br_ref[...])
    ig = _sigmoid(_dot(xcb, wi_ref[...]) + bi_ref[...])
    nl = -lam_ref[...]
    softplus = jnp.maximum(nl, 0.0) + jnp.log1p(jnp.exp(-jnp.abs(nl)))
    log_a = (-LRU_C) * r * softplus
    a = jnp.exp(log_a)
    th = jnp.tanh(log_a)
    u = jnp.sqrt(-2.0 * th / (1.0 - th)) * (ig * xc)

    row = lax.broadcasted_iota(I32, a.shape, 0) & (SUBLANES - 1)
    for d in (1, 2, 4):
        m = row >= d
        u = jnp.where(m, a * pltpu.roll(u, d, 0) + u, u)
        a = jnp.where(m, a * pltpu.roll(a, d, 0), a)
    a_s[...] = a
    u_s[...] = u

    def group(g, h):
        sl = pl.ds(pl.multiple_of(g * SUBLANES, SUBLANES), SUBLANES)
        hg = u_s[sl, :] + a_s[sl, :] * h
        u_s[sl, :] = hg
        return jnp.broadcast_to(hg[SUBLANES - 1:, :], hg.shape)

    h_ref[...] = lax.fori_loop(0, ts // SUBLANES, group, h_ref[...])

    gelu = 0.5 * gb * (1.0 + jnp.tanh(math.sqrt(2.0 / math.pi) * (gb + 0.044715 * (gb * gb * gb))))
    tok_ref[0] = u_s[...] * gelu


def _rglru_front(x, w_in, conv_w, conv_b, wr, br, wi, bi, lam, ts):
    b, s, d = x.shape
    tok_w = conv_w.shape[-1]
    mem_w = w_in.shape[-1] - 2 * tok_w
    const = lambda shape: pl.BlockSpec(shape, lambda i, j: (0,) * len(shape))
    return pl.pallas_call(
        functools.partial(_rglru_kernel, tok_w=tok_w),
        grid=(b, s // ts),
        in_specs=[pl.BlockSpec((1, ts, d), lambda i, j: (i, j, 0)),
                  const(w_in.shape), const(conv_w.shape), const(conv_b.shape),
                  const(wr.shape), const(br.shape), const(wi.shape), const(bi.shape),
                  const(lam.shape)],
        out_specs=[pl.BlockSpec((1, ts, tok_w), lambda i, j: (i, j, 0)),
                   pl.BlockSpec((1, ts, mem_w), lambda i, j: (i, j, 0))],
        out_shape=[jax.ShapeDtypeStruct((b, s, tok_w), F32),
                   jax.ShapeDtypeStruct((b, s, mem_w), F32)],
        scratch_shapes=[pltpu.VMEM((SUBLANES, tok_w), F32), pltpu.VMEM((SUBLANES, tok_w), F32),
                        pltpu.VMEM((ts, tok_w), F32), pltpu.VMEM((ts, tok_w), F32)],
        compiler_params=_params("arbitrary", "arbitrary"),
        name="rglru_front",
    )(x, w_in, conv_w, conv_b, wr, br, wi, bi, lam)


def _dsa_layout(tok_w):
    q0 = 0
    c0 = q0 + tok_w
    iq0 = c0 + KV_LATENT
    mq0 = iq0 + IDX_HEADS * IDX_DIM
    ik0 = mq0 + MEM_WIDTH
    iw0 = ik0 + LANES
    total = iw0 + LANES
    return q0, c0, iq0, mq0, ik0, iw0, total


def _dsa_proj_kernel(x_ref, w_ref, wuk_ref, kvg_ref, ig_ref, ib_ref,
                     qlat_ref, c_ref, iq_ref, ik_ref, iw_ref, mq_ref, *, tok_w):
    q0, c0, iq0, mq0, ik0, iw0, _ = _dsa_layout(tok_w)
    proj = _dot(x_ref[0].astype(BF16), w_ref[...])
    for h in range(tok_w // HEAD_DIM):
        qh = proj[:, q0 + h * HEAD_DIM:q0 + (h + 1) * HEAD_DIM].astype(BF16)
        ql = _dot(qh, wuk_ref[h]) * HEAD_DIM ** -0.5
        qlat_ref[0, :, h * KV_LATENT:(h + 1) * KV_LATENT] = ql.astype(BF16)
    c = proj[:, c0:c0 + KV_LATENT]
    c = c * lax.rsqrt(jnp.mean(c * c, axis=-1, keepdims=True) + RMS_EPS) * kvg_ref[...]
    c_ref[0] = c.astype(BF16)
    iq_ref[0] = proj[:, iq0:iq0 + IDX_HEADS * IDX_DIM].astype(BF16)
    ik = _layer_norm(proj[:, ik0:ik0 + IDX_DIM], ig_ref[...], ib_ref[...])
    ik_ref[0] = ik.astype(BF16)
    iw_ref[0] = proj[:, iw0:iw0 + IDX_HEADS] * (IDX_HEADS ** -0.5 * IDX_DIM ** -0.5)
    mq_ref[0] = proj[:, mq0:mq0 + MEM_WIDTH]


def _dsa_proj(x, w, wuk, kvg, ig, ib, ts):
    b, s, d = x.shape
    nh = wuk.shape[0]
    tok_w = nh * HEAD_DIM
    const = lambda shape: pl.BlockSpec(shape, lambda i, j: (0,) * len(shape))
    tile = lambda w_: pl.BlockSpec((1, ts, w_), lambda i, j: (i, j, 0))
    widths = [nh * KV_LATENT, KV_LATENT, IDX_HEADS * IDX_DIM, IDX_DIM, IDX_HEADS, MEM_WIDTH]
    dtypes = [BF16, BF16, BF16, BF16, F32, F32]
    return pl.pallas_call(
        functools.partial(_dsa_proj_kernel, tok_w=tok_w),
        grid=(b, s // ts),
        in_specs=[tile(d), const(w.shape), const(wuk.shape), const(kvg.shape),
                  const(ig.shape), const(ib.shape)],
        out_specs=[tile(w_) for w_ in widths],
        out_shape=[jax.ShapeDtypeStruct((b, s, w_), t) for w_, t in zip(widths, dtypes)],
        compiler_params=_params("arbitrary", "arbitrary"),
        name="dsa_proj",
    )(x, w, wuk, kvg, ig, ib)


def _dsa_select_kernel(iq_ref, iw_ref, ik_ref, mask_ref, keys_ref, j_ref, *, k_sel, chunk, seq):
    qi = pl.program_id(1)
    tq = iq_ref.shape[1]
    nch = ((qi + 1) * tq + chunk - 1) // chunk
    iq = iq_ref[0]
    iw = iw_ref[0]
    qpos = qi * tq + lax.broadcasted_iota(I32, (tq, chunk), 0)
    lane = lax.broadcasted_iota(I32, (tq, chunk), 1)

    def score(c, _):
        ikc = ik_ref[0, pl.ds(pl.multiple_of(c * chunk, chunk), chunk), :]
        sc = jnp.zeros((tq, chunk), F32)
        for h in range(IDX_HEADS):
            sh = _dot_nt(iq[:, h * IDX_DIM:(h + 1) * IDX_DIM], ikc)
            sc = sc + jnp.maximum(sh, 0.0) * iw[:, h:h + 1]
        sc = jnp.where(sc == 0.0, 0.0, sc)
        bits = lax.bitcast_convert_type(sc, I32)
        key = jnp.where(bits < 0, bits ^ 0x7FFFFFFF, bits)
        keys_ref[c] = jnp.where(c * chunk + lane <= qpos, key, INT_MIN)
        return 0

    lax.fori_loop(0, nch, score, 0)

    def count(pred):
        def body(c, acc):
            return acc + pred(keys_ref[c], c).astype(I32)
        acc = lax.fori_loop(0, nch, body, jnp.zeros((tq, chunk), I32))
        return jnp.sum(acc.astype(F32), axis=1, keepdims=True)

    def refine(t, cand):
        cb = jnp.broadcast_to(cand, (tq, chunk))
        cnt = count(lambda k, c: k >= cb)
        return jnp.where(cnt >= k_sel, cand, t)

    t0 = jnp.full((tq, 1), INT_MIN, I32)
    t = refine(t0, t0 ^ INT_MIN)

    def bit_step(i, t):
        return refine(t, t | (jnp.int32(1) << (30 - i)))

    t = lax.fori_loop(0, 31, bit_step, t)
    tb = jnp.broadcast_to(t, (tq, chunk))
    n_gt = count(lambda k, c: k > tb)
    n_eq = count(lambda k, c: k == tb)
    need = k_sel - n_gt

    j_ref[...] = jnp.full(j_ref.shape, seq - 1, I32)

    @pl.when(jnp.max(n_eq - need) > 0)
    def _():
        def idx_step(i, j):
            cand = j | (jnp.int32(1) << (seq.bit_length() - 2 - i))
            cb = jnp.broadcast_to(cand, (tq, chunk))
            cnt = count(lambda k, c: (k == tb) & (c * chunk + lane < cb))
            return jnp.where(cnt < need, cand, j)
        j_ref[...] = lax.fori_loop(0, seq.bit_length() - 1, idx_step, jnp.zeros((tq, 1), I32))

    jb = jnp.broadcast_to(j_ref[...], (tq, chunk))

    def pack(c, w):
        k = keys_ref[c]
        sel = ((k > tb) | ((k == tb) & (c * chunk + lane <= jb))) & (k != INT_MIN)
        return w | jnp.where(sel, jnp.int32(1) << c, 0)

    mask_ref[0] = lax.fori_loop(0, nch, pack, jnp.zeros((tq, chunk), I32))


def _dsa_select(iq, iw, ik, tq, chunk, k_sel):
    b, s, _ = iq.shape
    return pl.pallas_call(
        functools.partial(_dsa_select_kernel, k_sel=k_sel, chunk=chunk, seq=s),
        grid=(b, s // tq),
        in_specs=[pl.BlockSpec((1, tq, iq.shape[2]), lambda i, j: (i, j, 0)),
                  pl.BlockSpec((1, tq, iw.shape[2]), lambda i, j: (i, j, 0)),
                  pl.BlockSpec((1, s, ik.shape[2]), lambda i, j: (i, 0, 0))],
        out_specs=pl.BlockSpec((1, tq, chunk), lambda i, j: (i, j, 0)),
        out_shape=jax.ShapeDtypeStruct((b, s, chunk), I32),
        scratch_shapes=[pltpu.VMEM((MASK_BITS, tq, chunk), I32), pltpu.VMEM((tq, 1), I32)],
        compiler_params=_params("arbitrary", "arbitrary"),
        name="dsa_select",
    )(iq, iw, ik)


def _dsa_attend_kernel(qi_ref, kb_ref, qlat_ref, c_ref, mask_ref, bias_ref, wuv_ref, out_ref,
                       m_ref, l_ref, acc_ref, *, nh):
    p_id = pl.program_id(1)
    qi = qi_ref[p_id]
    kb = kb_ref[p_id]

    @pl.when(kb == 0)
    def _():
        m_ref[...] = jnp.full(m_ref.shape, NEG_BIG, F32)
        l_ref[...] = jnp.zeros_like(l_ref)
        acc_ref[...] = jnp.zeros_like(acc_ref)

    sel = ((mask_ref[0] >> kb) & 1) == 1
    c = c_ref[0]
    for h in range(nh):
        q = qlat_ref[0, :, h * KV_LATENT:(h + 1) * KV_LATENT]
        lg = jnp.where(sel, _dot_nt(q, c) + bias_ref[0, h], NEG_BIG)
        m_old = m_ref[h]
        m_new = jnp.maximum(m_old, jnp.max(lg, axis=1, keepdims=True))
        alpha = jnp.exp(m_old - m_new)
        p = jnp.exp(lg - m_new)
        l_ref[h] = alpha * l_ref[h] + jnp.sum(p, axis=1, keepdims=True)
        acc_ref[h] = alpha * acc_ref[h] + _dot(p.astype(BF16), c)
        m_ref[h] = m_new

    @pl.when(kb == qi)
    def _():
        for j in range(nh // 2):
            o = jnp.concatenate([acc_ref[2 * j] / l_ref[2 * j],
                                 acc_ref[2 * j + 1] / l_ref[2 * j + 1]], axis=1)
            out_ref[0, :, j * LANES:(j + 1) * LANES] = _dot(o.astype(BF16), wuv_ref[j])


def _dsa_attend(qlat, c, mask, bias, wuv2, tq):
    b, s, _ = qlat.shape
    nh = bias.shape[1]
    nq = s // tq
    qi_tab = np.concatenate([np.full(q + 1, q, np.int32) for q in range(nq)])
    kb_tab = np.concatenate([np.arange(q + 1, dtype=np.int32) for q in range(nq)])
    grid_spec = pltpu.PrefetchScalarGridSpec(
        num_scalar_prefetch=2,
        grid=(b, len(qi_tab)),
        in_specs=[pl.BlockSpec((1, tq, qlat.shape[2]), lambda i, p, qt, kt: (i, qt[p], 0)),
                  pl.BlockSpec((1, tq, c.shape[2]), lambda i, p, qt, kt: (i, kt[p], 0)),
                  pl.BlockSpec((1, tq, mask.shape[2]), lambda i, p, qt, kt: (i, qt[p], 0)),
                  pl.BlockSpec((1, nh, tq, tq),
                               lambda i, p, qt, kt: (jnp.minimum(qt[p] - kt[p], 2), 0, 0, 0)),
                  pl.BlockSpec(wuv2.shape, lambda i, p, qt, kt: (0, 0, 0))],
        out_specs=pl.BlockSpec((1, tq, nh * HEAD_DIM), lambda i, p, qt, kt: (i, qt[p], 0)),
        scratch_shapes=[pltpu.VMEM((nh, tq, 1), F32), pltpu.VMEM((nh, tq, 1), F32),
                        pltpu.VMEM((nh, tq, KV_LATENT), F32)],
    )
    return pl.pallas_call(
        functools.partial(_dsa_attend_kernel, nh=nh),
        grid_spec=grid_spec,
        out_shape=jax.ShapeDtypeStruct((b, s, nh * HEAD_DIM), F32),
        compiler_params=_params("arbitrary", "arbitrary"),
        name="dsa_attend",
    )(jnp.asarray(qi_tab), jnp.asarray(kb_tab), qlat, c, mask, bias, wuv2)


def _t5_bucket(rel):
    n = jnp.maximum(rel, 0)
    max_exact = REL_BUCKETS // 2
    large = max_exact + (jnp.log(jnp.maximum(n, 1).astype(F32) / max_exact)
                         / math.log(REL_MAX_DIST / max_exact) * (REL_BUCKETS - max_exact)).astype(I32)
    large = jnp.minimum(large, REL_BUCKETS - 1)
    return jnp.where(n < max_exact, n, large)


def _bias_tiles(rel_bias, tq):
    assert tq >= REL_MAX_DIST
    i = jnp.arange(tq, dtype=I32)
    rel0 = i[:, None] - i[None, :]
    last = rel_bias[REL_BUCKETS - 1]
    tabs = [rel_bias[_t5_bucket(rel0 + d * tq)] - last for d in range(3)]
    return jnp.stack([jnp.moveaxis(t, -1, 0) for t in tabs])


def _post_kernel(x_ref, tok_ref, mq_ref, kv_ref, wout_ref, g_ref, b_ref, x1_ref, *, tok_w):
    mq = mq_ref[0]
    kv = kv_ref[0]
    y = _dot(tok_ref[0].astype(BF16), wout_ref[:tok_w, :])
    for h in range(MEM_HEADS):
        q = mq[:, h * HEAD_DIM:(h + 1) * HEAD_DIM].astype(BF16)
        k = kv[:, h * HEAD_DIM:(h + 1) * HEAD_DIM]
        v = kv[:, MEM_WIDTH + h * HEAD_DIM:MEM_WIDTH + (h + 1) * HEAD_DIM]
        lg = _dot_nt(q, k) * HEAD_DIM ** -0.5
        p = jnp.exp(lg - jnp.max(lg, axis=1, keepdims=True))
        p = p / jnp.sum(p, axis=1, keepdims=True)
        o = _dot(p.astype(BF16), v)
        y = y + _dot(o.astype(BF16), wout_ref[tok_w + h * HEAD_DIM:tok_w + (h + 1) * HEAD_DIM, :])
    x1_ref[0] = _layer_norm(DN_ALPHA * x_ref[0] + y, g_ref[...], b_ref[...])


def _post_mixer(x, tok, mq, kv, w_out, g, b, ts):
    bsz, s, d = x.shape
    tok_w = tok.shape[2]
    const = lambda shape: pl.BlockSpec(shape, lambda i, j: (0,) * len(shape))
    tile = lambda w_: pl.BlockSpec((1, ts, w_), lambda i, j: (i, j, 0))
    return pl.pallas_call(
        functools.partial(_post_kernel, tok_w=tok_w),
        grid=(bsz, s // ts),
        in_specs=[tile(d), tile(tok_w), tile(mq.shape[2]),
                  pl.BlockSpec((1,) + kv.shape[1:], lambda i, j: (i, 0, 0)),
                  const(w_out.shape), const(g.shape), const(b.shape)],
        out_specs=tile(d),
        out_shape=jax.ShapeDtypeStruct((bsz, s, d), F32),
        compiler_params=_params("arbitrary", "arbitrary"),
        name="post_mixer",
    )(x, tok, mq, kv, w_out, g, b)


def _router_kernel(x_ref, wh_ref, wl_ref, rb_ref, e_ref, gate_ref, rank_ref, cnt_ref, carry_ref):
    @pl.when(pl.program_id(0) == 0)
    def _():
        carry_ref[...] = jnp.zeros_like(carry_ref)

    x = x_ref[...]
    tr = x.shape[0]
    xh = x.astype(BF16)
    xl = (x - xh.astype(F32)).astype(BF16)
    lg = _dot(xh, wh_ref[...]) + (_dot(xl, wh_ref[...]) + _dot(xh, wl_ref[...])) + rb_ref[...]
    lane = lax.broadcasted_iota(I32, lg.shape, 1).astype(F32)
    lg = jnp.where(lane < N_EXPERTS, lg, -jnp.inf)

    idx, val = [], []
    onehot = jnp.zeros(lg.shape, F32)
    for _ in range(TOP_K):
        m = jnp.max(lg, axis=1, keepdims=True)
        i = jnp.min(jnp.where(lg == m, lane, float(LANES)), axis=1, keepdims=True)
        hit = lane == i
        idx.append(i)
        val.append(m)
        onehot = jnp.where(hit, 1.0, onehot)
        lg = jnp.where(hit, -jnp.inf, lg)

    ex = [jnp.exp(v - val[0]) for v in val]
    den = ex[0] + ex[1] + ex[2] + ex[3]

    r_i = lax.broadcasted_iota(I32, (tr, tr), 0)
    c_i = lax.broadcasted_iota(I32, (tr, tr), 1)
    tri = jnp.where(c_i < r_i, 1.0, 0.0).astype(BF16)
    before = _dot(tri, onehot.astype(BF16)) + carry_ref[...]
    carry_ref[...] = carry_ref[...] + jnp.sum(onehot, axis=0, keepdims=True)
    cnt_ref[...] = carry_ref[...]

    lane_k = lax.broadcasted_iota(I32, (tr, TOP_K), 1)
    e_out = jnp.zeros((tr, TOP_K), I32)
    g_out = jnp.zeros((tr, TOP_K), F32)
    r_out = jnp.zeros((tr, TOP_K), I32)
    for k in range(TOP_K):
        rk = jnp.sum(jnp.where(lane == idx[k], before, 0.0), axis=1, keepdims=True).astype(I32)
        e_out = jnp.where(lane_k == k, idx[k].astype(I32), e_out)
        g_out = jnp.where(lane_k == k, ex[k] / den, g_out)
        r_out = jnp.where(lane_k == k, rk, r_out)
    e_ref[...] = e_out
    gate_ref[...] = g_out
    rank_ref[...] = r_out


def _router(x, wh, wl, rb, tr):
    t, d = x.shape
    const = lambda shape: pl.BlockSpec(shape, lambda i: (0,) * len(shape))
    small = pl.BlockSpec((tr, TOP_K), lambda i: (i, 0))
    return pl.pallas_call(
        _router_kernel,
        grid=(t // tr,),
        in_specs=[pl.BlockSpec((tr, d), lambda i: (i, 0)), const(wh.shape), const(wl.shape),
                  const(rb.shape)],
        out_specs=[small, small, small, const((1, LANES))],
        out_shape=[jax.ShapeDtypeStruct((t, TOP_K), I32), jax.ShapeDtypeStruct((t, TOP_K), F32),
                   jax.ShapeDtypeStruct((t, TOP_K), I32), jax.ShapeDtypeStruct((1, LANES), F32)],
        scratch_shapes=[pltpu.VMEM((1, LANES), F32)],
        compiler_params=_params("arbitrary"),
        name="router",
    )(x, wh, wl, rb)


def _dest_kernel(e_ref, rank_ref, start_ref, dest_ref):
    e = e_ref[...]
    lane = lax.broadcasted_iota(I32, (e.shape[0], LANES), 1)
    lane_k = lax.broadcasted_iota(I32, e.shape, 1)
    out = rank_ref[...]
    for k in range(TOP_K):
        st = jnp.sum(jnp.where(lane == e[:, k:k + 1], start_ref[...], 0.0), axis=1, keepdims=True)
        out = out + jnp.where(lane_k == k, st.astype(I32), 0)
    dest_ref[...] = out


def _dest(e, rank, start, tr):
    t = e.shape[0]
    small = pl.BlockSpec((tr, TOP_K), lambda i: (i, 0))
    return pl.pallas_call(
        _dest_kernel,
        grid=(t // tr,),
        in_specs=[small, small, pl.BlockSpec((1, LANES), lambda i: (0, 0))],
        out_specs=small,
        out_shape=jax.ShapeDtypeStruct((t, TOP_K), I32),
        compiler_params=_params("arbitrary"),
        name="dest",
    )(e, rank, start)


def _dispatch_kernel(dest_ref, x_ref, xs_in_ref, xs_ref, sem):
    del xs_in_ref
    tt = x_ref.shape[0]

    def row_copy(i, k):
        d = dest_ref[i * TOP_K + k]
        return pltpu.make_async_copy(x_ref.at[pl.ds(i, 1)], xs_ref.at[pl.ds(d, 1)], sem)

    def issue(i, _):
        for k in range(TOP_K):
            row_copy(i, k).start()
        return 0

    def drain(i, _):
        for k in range(TOP_K):
            row_copy(i, k).wait()
        return 0

    lax.fori_loop(0, tt, issue, 0)
    lax.fori_loop(0, tt, drain, 0)


def _dispatch(dest_flat, x, rows, tt):
    t, d = x.shape
    xs0 = jnp.zeros((rows, d), x.dtype)
    return pl.pallas_call(
        _dispatch_kernel,
        grid=(t // tt,),
        in_specs=[pl.BlockSpec((tt * TOP_K,), lambda i: (i,), memory_space=pltpu.SMEM),
                  pl.BlockSpec((tt, d), lambda i: (i, 0)),
                  pl.BlockSpec(memory_space=pl.ANY)],
        out_specs=pl.BlockSpec(memory_space=pl.ANY),
        out_shape=jax.ShapeDtypeStruct((rows, d), x.dtype),
        scratch_shapes=[pltpu.SemaphoreType.DMA(())],
        input_output_aliases={2: 0},
        compiler_params=_params("arbitrary"),
        name="dispatch",
    )(dest_flat, x, xs0)


def _ffn_kernel(be_ref, xs_ref, w1_ref, b1_ref, w2_ref, b2_ref, ys_ref):
    del be_ref
    de = w2_ref.shape[1]
    h = _dot(xs_ref[...].astype(BF16), w1_ref[0]) + b1_ref[0]
    gt = jnp.minimum(h[:, :de], SWIGLU_LIMIT)
    up = jnp.clip(h[:, de:], -SWIGLU_LIMIT, SWIGLU_LIMIT)
    act = (up + 1.0) * (gt * _sigmoid(SWIGLU_ALPHA * gt))
    ys_ref[...] = _dot(act.astype(BF16), w2_ref[0]) + b2_ref[0]


def _expert_ffn(blk_e, xs, w1, b1, w2, b2, tm):
    rows, d = xs.shape
    grid_spec = pltpu.PrefetchScalarGridSpec(
        num_scalar_prefetch=1,
        grid=(rows // tm,),
        in_specs=[pl.BlockSpec((tm, d), lambda i, be: (i, 0)),
                  pl.BlockSpec((1,) + w1.shape[1:], lambda i, be: (be[i], 0, 0)),
                  pl.BlockSpec((1,) + b1.shape[1:], lambda i, be: (be[i], 0, 0)),
                  pl.BlockSpec((1,) + w2.shape[1:], lambda i, be: (be[i], 0, 0)),
                  pl.BlockSpec((1,) + b2.shape[1:], lambda i, be: (be[i], 0, 0))],
        out_specs=pl.BlockSpec((tm, d), lambda i, be: (i, 0)),
    )
    return pl.pallas_call(
        _ffn_kernel,
        grid_spec=grid_spec,
        out_shape=jax.ShapeDtypeStruct((rows, d), F32),
        compiler_params=_params("arbitrary"),
        name="expert_ffn",
    )(blk_e, xs, w1, b1, w2, b2)


def _combine_kernel(dest_ref, x_ref, gate_ref, g_ref, b_ref, ys_ref, out_ref, buf, sem):
    tt = x_ref.shape[0]

    def row_copy(i, k):
        d = dest_ref[i * TOP_K + k]
        return pltpu.make_async_copy(ys_ref.at[pl.ds(d, 1)], buf.at[k, pl.ds(i, 1)], sem)

    def issue(i, _):
        for k in range(TOP_K):
            row_copy(i, k).start()
        return 0

    def drain(i, _):
        for k in range(TOP_K):
            row_copy(i, k).wait()
        return 0

    lax.fori_loop(0, tt, issue, 0)
    lax.fori_loop(0, tt, drain, 0)

    gate = gate_ref[...]
    ffn = gate[:, 0:1] * buf[0]
    for k in range(1, TOP_K):
        ffn = ffn + gate[:, k:k + 1] * buf[k]
    out_ref[...] = _layer_norm(DN_ALPHA * x_ref[...] + ffn, g_ref[...], b_ref[...])


def _combine(dest_flat, x, gate, g, b, ys, tt):
    t, d = x.shape
    const = lambda shape: pl.BlockSpec(shape, lambda i: (0,) * len(shape))
    return pl.pallas_call(
        _combine_kernel,
        grid=(t // tt,),
        in_specs=[pl.BlockSpec((tt * TOP_K,), lambda i: (i,), memory_space=pltpu.SMEM),
                  pl.BlockSpec((tt, d), lambda i: (i, 0)),
                  pl.BlockSpec((tt, TOP_K), lambda i: (i, 0)),
                  const(g.shape), const(b.shape),
                  pl.BlockSpec(memory_space=pl.ANY)],
        out_specs=pl.BlockSpec((tt, d), lambda i: (i, 0)),
        out_shape=jax.ShapeDtypeStruct((t, d), F32),
        scratch_shapes=[pltpu.VMEM((TOP_K, tt, d), F32), pltpu.SemaphoreType.DMA(())],
        compiler_params=_params("arbitrary"),
        name="combine",
    )(dest_flat, x, gate, g, b, ys)


def _moe_layer(x, router_w, router_b, w1, b1, w2, b2, g, b, tiles):
    t, d = x.shape
    tm, tt, tr = tiles["tm"], tiles["tt"], tiles["tr"]
    wpad = jnp.pad(router_w, ((0, 0), (0, LANES - N_EXPERTS)))
    wh = wpad.astype(BF16)
    wl = (wpad - wh.astype(F32)).astype(BF16)
    rb = jnp.pad(router_b, (0, LANES - N_EXPERTS))[None, :]
    e, gate, rank, cnt = _router(x, wh, wl, rb, tr)

    counts = cnt[0, :N_EXPERTS].astype(I32)
    padded = (counts + tm - 1) // tm * tm
    pend = jnp.cumsum(padded)
    start = jnp.pad(pend - padded, (0, LANES - N_EXPERTS))[None, :].astype(F32)
    ntile = t * TOP_K // tm + N_EXPERTS
    blk_e = jnp.minimum(jnp.searchsorted(pend, jnp.arange(ntile, dtype=I32) * tm, side='right'),
                        N_EXPERTS - 1).astype(I32)

    dest = _dest(e, rank, start, tr).reshape(-1)
    xs = _dispatch(dest, x, ntile * tm, tt)
    ys = _expert_ffn(blk_e, xs, w1.astype(BF16), b1[:, None, :], w2.astype(BF16), b2[:, None, :], tm)
    return _combine(dest, x, gate, g[None, :], b[None, :], ys, tt)


def _block_diag(w):
    n, c, _ = w.shape
    eye = jnp.eye(n, dtype=w.dtype)
    return (eye[:, None, :, None] * w[:, :, None, :]).reshape(n * c, n * c)


def kernel(x, mem, rel_bias, a_w_in, a_conv_w, a_conv_b, a_wr, a_br, a_wi, a_bi, a_lambda, b_w_in, b_kv_norm_g, b_w_uk, b_w_uv, b_idx_norm_g, b_idx_norm_b, w_mem_kv, w_out, ln1_g, ln1_b, router_w, router_b, exp_w1, exp_b1, exp_w2, exp_b2, ln2_g, ln2_b):
    bsz, seq, d = x.shape
    tiles = _tiles(seq)
    ts = tiles["ts"]
    row = lambda v: v[None, :]

    def finish_layer(layer, x, tok, mq):
        kv = _matmul(mem.reshape(-1, d), w_mem_kv[layer].astype(BF16), mem.shape[1])
        kv = kv.astype(BF16).reshape(bsz, mem.shape[1], -1)
        x1 = _post_mixer(x, tok, mq, kv, w_out[layer].astype(BF16), row(ln1_g[layer]),
                         row(ln1_b[layer]), ts)
        x2 = _moe_layer(x1.reshape(-1, d), router_w[layer], router_b[layer], exp_w1[layer],
                        exp_b1[layer], exp_w2[layer], exp_b2[layer], ln2_g[layer], ln2_b[layer],
                        tiles)
        return x2.reshape(bsz, seq, d)

    tok, mq = _rglru_front(x, a_w_in[0].astype(BF16), a_conv_w[0], row(a_conv_b[0]),
                           _block_diag(a_wr[0]).astype(BF16), row(a_br[0]),
                           _block_diag(a_wi[0]).astype(BF16), row(a_bi[0]), row(a_lambda[0]), ts)
    x = finish_layer(0, x, tok, mq)

    nh = b_w_uk.shape[2]
    tok_w = nh * HEAD_DIM
    q0, c0, iq0, mq0, ik0, iw0, total = _dsa_layout(tok_w)
    w = b_w_in[0]
    o_c, o_iq = tok_w, tok_w + KV_LATENT
    o_ik = o_iq + IDX_HEADS * IDX_DIM
    o_iw = o_ik + IDX_DIM
    o_mq = o_iw + IDX_HEADS
    w_re = jnp.zeros((d, total), F32)
    w_re = w_re.at[:, q0:q0 + tok_w].set(w[:, :tok_w])
    w_re = w_re.at[:, c0:c0 + KV_LATENT].set(w[:, o_c:o_iq])
    w_re = w_re.at[:, iq0:iq0 + IDX_HEADS * IDX_DIM].set(w[:, o_iq:o_ik])
    w_re = w_re.at[:, ik0:ik0 + IDX_DIM].set(w[:, o_ik:o_iw])
    w_re = w_re.at[:, iw0:iw0 + IDX_HEADS].set(w[:, o_iw:o_mq])
    w_re = w_re.at[:, mq0:mq0 + MEM_WIDTH].set(w[:, o_mq:])
    wuk = jnp.transpose(b_w_uk[0], (1, 2, 0)).astype(BF16)
    wuv = jnp.transpose(b_w_uv[0], (1, 0, 2))
    wuv2 = jnp.zeros((nh // 2, 2 * KV_LATENT, 2 * HEAD_DIM), F32)
    wuv2 = wuv2.at[:, :KV_LATENT, :HEAD_DIM].set(wuv[0::2])
    wuv2 = wuv2.at[:, KV_LATENT:, HEAD_DIM:].set(wuv[1::2]).astype(BF16)
    qlat, c, iq, ik, iw, mq = _dsa_proj(x, w_re.astype(BF16), wuk, row(b_kv_norm_g[0]),
                                        row(b_idx_norm_g[0]), row(b_idx_norm_b[0]), ts)
    k_sel = min(TOPK_MAX, seq // 4)
    mask = _dsa_select(iq, iw, ik, tiles["tq_sel"], tiles["chunk"], k_sel)
    tok = _dsa_attend(qlat, c, mask, _bias_tiles(rel_bias, tiles["tq_att"]), wuv2, tiles["tq_att"])
    x = finish_layer(1, x, tok, mq)
    return x
```

```python
import functools
import math

import jax
import jax.numpy as jnp
import numpy as np
from jax import lax
from jax.experimental import pallas as pl
from jax.experimental.pallas import tpu as pltpu

F32 = jnp.float32
BF16 = jnp.bfloat16
I32 = jnp.int32

HEAD_DIM = 64
MEM_HEADS = 4
MEM_WIDTH = MEM_HEADS * HEAD_DIM
LRU_C = 8.0
CONV_W = 4
KV_LATENT = 128
IDX_HEADS = 4
IDX_DIM = 64
TOPK_MAX = 256
REL_BUCKETS = 32
REL_MAX_DIST = 128
N_EXPERTS = 32
TOP_K = 4
SWIGLU_LIMIT = 7.0
SWIGLU_ALPHA = 1.702
DEPTH = 2
DN_ALPHA = (2 * DEPTH) ** 0.25
LN_EPS = 1e-5
RMS_EPS = 1e-6

SUBLANES = 8
LANES = 128
MASK_BITS = 32
INT_MIN = -2 ** 31
NEG_BIG = -1e30
VMEM_LIMIT = 56 * 1024 * 1024


def _tiles(seq):
    chunk = seq // MASK_BITS
    return dict(
        ts=min(512, seq),
        chunk=chunk,
        tq_att=chunk,
        tt=256,
        tm=256,
        tr=512,
    )


def _dot(a, b):
    return jnp.dot(a, b, preferred_element_type=F32)


def _dot_nt(a, b):
    return lax.dot_general(a, b, (((1,), (1,)), ((), ())), preferred_element_type=F32)


def _sigmoid(x):
    return 1.0 / (1.0 + jnp.exp(-x))


def _layer_norm(z, g, b):
    mu = jnp.mean(z, axis=-1, keepdims=True)
    zc = z - mu
    var = jnp.mean(zc * zc, axis=-1, keepdims=True)
    return zc * lax.rsqrt(var + LN_EPS) * g + b


def _params(*sem):
    return pltpu.CompilerParams(dimension_semantics=sem, vmem_limit_bytes=VMEM_LIMIT)


def _mm_kernel(a_ref, b_ref, o_ref):
    o_ref[...] = _dot(a_ref[...].astype(BF16), b_ref[...])


def _matmul(a, b, tm):
    m, k = a.shape
    n = b.shape[1]
    return pl.pallas_call(
        _mm_kernel,
        grid=(m // tm,),
        in_specs=[pl.BlockSpec((tm, k), lambda i: (i, 0)),
                  pl.BlockSpec((k, n), lambda i: (0, 0))],
        out_specs=pl.BlockSpec((tm, n), lambda i: (i, 0)),
        out_shape=jax.ShapeDtypeStruct((m, n), F32),
        compiler_params=_params("arbitrary"),
        name="matmul",
    )(a, b)


def _rglru_kernel(x_ref, win_ref, cw_ref, cb_ref, wr_ref, br_ref, wi_ref, bi_ref, lam_ref,
                  tok_ref, mq_ref, tail_ref, h_ref, a_s, u_s, *, tok_w):
    s = pl.program_id(1)

    @pl.when(s == 0)
    def _():
        tail_ref[...] = jnp.zeros_like(tail_ref)
        h_ref[...] = jnp.zeros_like(h_ref)

    proj = _dot(x_ref[0].astype(BF16), win_ref[...])
    xb = proj[:, :tok_w]
    gb = proj[:, tok_w:2 * tok_w]
    mq_ref[0] = proj[:, 2 * tok_w:]
    ts = xb.shape[0]

    xe = jnp.concatenate([tail_ref[...], xb], axis=0)
    tail_ref[...] = xb[ts - SUBLANES:, :]
    xc = xb * cw_ref[CONV_W - 1:CONV_W, :] + cb_ref[...]
    for d in range(1, CONV_W):
        sh = pltpu.roll(xe, d, 0)[SUBLANES:, :]
        xc = xc + sh * cw_ref[CONV_W - 1 - d:CONV_W - d, :]

    xcb = xc.astype(BF16)
    r = _sigmoid(_dot(xcb, wr_ref[...]) + br_ref[...])
    ig = _sigmoid(_dot(xcb, wi_ref[...]) + bi_ref[...])
    nl = -lam_ref[...]
    softplus = jnp.maximum(nl, 0.0) + jnp.log1p(jnp.exp(-jnp.abs(nl)))
    log_a = (-LRU_C) * r * softplus
    a = jnp.exp(log_a)
    th = jnp.tanh(log_a)
    u = jnp.sqrt(-2.0 * th / (1.0 - th)) * (ig * xc)

    row = lax.broadcasted_iota(I32, a.shape, 0) & (SUBLANES - 1)
    for d in (1, 2, 4):
        m = row >= d
        u = jnp.where(m, a * pltpu.roll(u, d, 0) + u, u)
        a = jnp.where(m, a * pltpu.roll(a, d, 0), a)
    a_s[...] = a
    u_s[...] = u

    def group(g, h):
        sl = pl.ds(pl.multiple_of(g * SUBLANES, SUBLANES), SUBLANES)
        hg = u_s[sl, :] + a_s[sl, :] * h
        u_s[sl, :] = hg
        return jnp.broadcast_to(hg[SUBLANES - 1:, :], hg.shape)

    h_ref[...] = lax.fori_loop(0, ts // SUBLANES, group, h_ref[...])

    gelu = 0.5 * gb * (1.0 + jnp.tanh(math.sqrt(2.0 / math.pi) * (gb + 0.044715 * (gb * gb * gb))))
    tok_ref[0] = u_s[...] * gelu


def _rglru_front(x, w_in, conv_w, conv_b, wr, br, wi, bi, lam, ts):
    b, s, d = x.shape
    tok_w = conv_w.shape[-1]
    mem_w = w_in.shape[-1] - 2 * tok_w
    const = lambda shape: pl.BlockSpec(shape, lambda i, j: (0,) * len(shape))
    return pl.pallas_call(
        functools.partial(_rglru_kernel, tok_w=tok_w),
        grid=(b, s // ts),
        in_specs=[pl.BlockSpec((1, ts, d), lambda i, j: (i, j, 0)),
                  const(w_in.shape), const(conv_w.shape), const(conv_b.shape),
                  const(wr.shape), const(br.shape), const(wi.shape), const(bi.shape),
                  const(lam.shape)],
        out_specs=[pl.BlockSpec((1, ts, tok_w), lambda i, j: (i, j, 0)),
                   pl.BlockSpec((1, ts, mem_w), lambda i, j: (i, j, 0))],
        out_shape=[jax.ShapeDtypeStruct((b, s, tok_w), F32),
                   jax.ShapeDtypeStruct((b, s, mem_w), F32)],
        scratch_shapes=[pltpu.VMEM((SUBLANES, tok_w), F32), pltpu.VMEM((SUBLANES, tok_w), F32),
                        pltpu.VMEM((ts, tok_w), F32), pltpu.VMEM((ts, tok_w), F32)],
        compiler_params=_params("arbitrary", "arbitrary"),
        name="rglru_front",
    )(x, w_in, conv_w, conv_b, wr, br, wi, bi, lam)


def _dsa_layout(tok_w):
    q0 = 0
    c0 = q0 + tok_w
    iq0 = c0 + KV_LATENT
    mq0 = iq0 + IDX_HEADS * IDX_DIM
    ik0 = mq0 + MEM_WIDTH
    iw0 = ik0 + LANES
    total = iw0 + LANES
    return q0, c0, iq0, mq0, ik0, iw0, total


def _dsa_proj_kernel(x_ref, w_ref, wuk_ref, kvg_ref, ig_ref, ib_ref,
                     qlat_ref, c_ref, iq_ref, ik_ref, iw_ref, mq_ref, *, tok_w):
    q0, c0, iq0, mq0, ik0, iw0, _ = _dsa_layout(tok_w)
    proj = _dot(x_ref[0].astype(BF16), w_ref[...])
    for h in range(tok_w // HEAD_DIM):
        qh = proj[:, q0 + h * HEAD_DIM:q0 + (h + 1) * HEAD_DIM].astype(BF16)
        ql = _dot(qh, wuk_ref[h]) * HEAD_DIM ** -0.5
        qlat_ref[0, h] = ql.astype(BF16)
    c = proj[:, c0:c0 + KV_LATENT]
    c = c * lax.rsqrt(jnp.mean(c * c, axis=-1, keepdims=True) + RMS_EPS) * kvg_ref[...]
    c_ref[0] = c.astype(BF16)
    iq_ref[0] = proj[:, iq0:iq0 + IDX_HEADS * IDX_DIM].astype(BF16)
    ik = _layer_norm(proj[:, ik0:ik0 + IDX_DIM], ig_ref[...], ib_ref[...])
    ik_ref[0] = ik.astype(BF16)
    iw_ref[0] = proj[:, iw0:iw0 + IDX_HEADS] * (IDX_HEADS ** -0.5 * IDX_DIM ** -0.5)
    mq_ref[0] = proj[:, mq0:mq0 + MEM_WIDTH]


def _dsa_proj(x, w, wuk, kvg, ig, ib, ts):
    b, s, d = x.shape
    nh = wuk.shape[0]
    tok_w = nh * HEAD_DIM
    const = lambda shape: pl.BlockSpec(shape, lambda i, j: (0,) * len(shape))
    tile = lambda w_: pl.BlockSpec((1, ts, w_), lambda i, j: (i, j, 0))
    widths = [KV_LATENT, IDX_HEADS * IDX_DIM, IDX_DIM, IDX_HEADS, MEM_WIDTH]
    dtypes = [BF16, BF16, BF16, F32, F32]
    return pl.pallas_call(
        functools.partial(_dsa_proj_kernel, tok_w=tok_w),
        grid=(b, s // ts),
        in_specs=[tile(d), const(w.shape), const(wuk.shape), const(kvg.shape),
                  const(ig.shape), const(ib.shape)],
        out_specs=[pl.BlockSpec((1, nh, ts, KV_LATENT), lambda i, j: (i, 0, j, 0))]
        + [tile(w_) for w_ in widths],
        out_shape=[jax.ShapeDtypeStruct((b, nh, s, KV_LATENT), BF16)]
        + [jax.ShapeDtypeStruct((b, s, w_), t) for w_, t in zip(widths, dtypes)],
        compiler_params=_params("arbitrary", "arbitrary"),
        name="dsa_proj",
    )(x, w, wuk, kvg, ig, ib)


def _row_popcount(words):
    pc = lax.population_count(words)
    part = pc[:, :LANES]
    for i in range(1, words.shape[1] // LANES):
        part = part + pc[:, i * LANES:(i + 1) * LANES]
    return jnp.sum(part.astype(F32), axis=1, keepdims=True)


def _dsa_select_kernel(iq_ref, iw_ref, ikt_ref, mask_ref, keys_ref, iwb_ref, tie_ref,
                       *, k_sel, chunk):
    qi = pl.program_id(1)
    tq = iq_ref.shape[1]
    iq = iq_ref[0]
    iw = iw_ref[0]
    for h in range(IDX_HEADS):
        iwb_ref[h] = jnp.broadcast_to(iw[:, h:h + 1], (tq, chunk))
    iqh = [iq[:, h * IDX_DIM:(h + 1) * IDX_DIM] for h in range(IDX_HEADS)]

    def ukey(c):
        ikc = ikt_ref[0, c]
        sc = jnp.maximum(_dot(iqh[0], ikc), 0.0) * iwb_ref[0]
        for h in range(1, IDX_HEADS):
            sc = sc + jnp.maximum(_dot(iqh[h], ikc), 0.0) * iwb_ref[h]
        sc = jnp.where(sc == 0.0, 0.0, sc)
        bits = lax.bitcast_convert_type(sc, I32)
        return jnp.where(bits < 0, ~bits, bits ^ INT_MIN)

    def score(c, _):
        keys_ref[c] = ukey(c)
        return 0

    lax.fori_loop(0, qi, score, 0)
    row = lax.broadcasted_iota(I32, (tq, chunk), 0)
    lane = lax.broadcasted_iota(I32, (tq, chunk), 1)
    keys_ref[qi] = jnp.where(lane <= row, ukey(qi), 0)

    def clear(c, _):
        keys_ref[c] = jnp.zeros((tq, chunk), I32)
        return 0

    lax.fori_loop(qi + 1, MASK_BITS, clear, 0)

    def transpose(g, _):
        rows = pl.ds(pl.multiple_of(g * SUBLANES, SUBLANES), SUBLANES)
        for half in range(chunk // LANES):
            cols = slice(half * LANES, (half + 1) * LANES)
            a = [keys_ref[c, rows, cols] for c in range(MASK_BITS)]
            j, m = MASK_BITS // 2, 0x0000FFFF
            while j:
                k = 0
                while k < MASK_BITS:
                    t = (lax.shift_right_logical(a[k], j) ^ a[k + j]) & m
                    a[k] = a[k] ^ (t << j)
                    a[k + j] = a[k + j] ^ t
                    k = (k + j + 1) & ~j
                j >>= 1
                m = (m ^ (m << j)) & 0xFFFFFFFF
                m = m - (1 << 32) if m >= (1 << 31) else m
            for c in range(MASK_BITS):
                keys_ref[c, rows, cols] = a[c]
        return 0

    lax.fori_loop(0, tq // SUBLANES, transpose, 0)

    def plane_step(i, carry):
        active, greater, need, any_one = carry
        ones = active & keys_ref[MASK_BITS - 1 - i]
        cnt = _row_popcount(ones)
        take = cnt >= need
        active = jnp.where(take, ones, active & ~keys_ref[MASK_BITS - 1 - i])
        greater = jnp.where(take, greater, greater | ones)
        need = jnp.where(take, need, need - cnt)
        return active, greater, need, jnp.where(take, 1.0, any_one)

    full = jnp.full((tq, chunk), -1, I32)
    ties, greater, need, any_one = lax.fori_loop(
        0, MASK_BITS, plane_step,
        (full, jnp.zeros((tq, chunk), I32), jnp.full((tq, 1), float(k_sel), F32),
         jnp.zeros((tq, 1), F32)))
    ties = jnp.where(any_one > 0.0, ties, 0)
    tie_ref[...] = ties

    @pl.when(jnp.max(_row_popcount(ties) - need) > 0.0)
    def _():
        shift = chunk.bit_length() - 1
        lane_i = lax.broadcasted_iota(I32, (tq, chunk), 1)

        def below(v, inclusive):
            vh = jnp.broadcast_to(v >> shift, (tq, chunk))
            vl = jnp.broadcast_to(v & (chunk - 1), (tq, chunk))
            top = jnp.int32(1) << vh
            edge = (lane_i <= vl) if inclusive else (lane_i < vl)
            return (top - 1) | jnp.where(edge, top, 0)

        nbits = (MASK_BITS * chunk).bit_length() - 1

        def idx_step(i, j):
            cand = j | (jnp.int32(1) << (nbits - 1 - i))
            cnt = _row_popcount(ties & below(cand, False))
            return jnp.where(cnt < need, cand, j)

        last = lax.fori_loop(0, nbits, idx_step, jnp.zeros((tq, 1), I32))
        tie_ref[...] = ties & below(last, True)

    mask_ref[0] = greater | tie_ref[...]


def _dsa_select(iq, iw, ikt, chunk, k_sel):
    b, s, _ = iq.shape
    tq = chunk
    return pl.pallas_call(
        functools.partial(_dsa_select_kernel, k_sel=k_sel, chunk=chunk),
        grid=(b, s // tq),
        in_specs=[pl.BlockSpec((1, tq, iq.shape[2]), lambda i, j: (i, j, 0)),
                  pl.BlockSpec((1, tq, iw.shape[2]), lambda i, j: (i, j, 0)),
                  pl.BlockSpec((1,) + ikt.shape[1:], lambda i, j: (i, 0, 0, 0))],
        out_specs=pl.BlockSpec((1, tq, chunk), lambda i, j: (i, j, 0)),
        out_shape=jax.ShapeDtypeStruct((b, s, chunk), I32),
        scratch_shapes=[pltpu.VMEM((MASK_BITS, tq, chunk), I32),
                        pltpu.VMEM((IDX_HEADS, tq, chunk), F32),
                        pltpu.VMEM((tq, chunk), I32)],
        compiler_params=_params("arbitrary", "arbitrary"),
        name="dsa_select",
    )(iq, iw, ikt)


def _dsa_attend_kernel(qi_ref, kb_ref, qlat_ref, c_ref, mask_ref, bias_ref, wuv_ref, out_ref,
                       m_ref, acc_ref, madd_ref, *, nh, rg):
    p_id = pl.program_id(1)
    qi = qi_ref[p_id]
    kb = kb_ref[p_id]
    tq, tk = madd_ref.shape
    ngroups = tq // rg

    @pl.when(kb == 0)
    def _():
        m_ref[...] = jnp.full(m_ref.shape, NEG_BIG, F32)
        acc_ref[...] = jnp.zeros_like(acc_ref)

    madd_ref[...] = jnp.where(((mask_ref[0] >> kb) & 1) == 1, 0.0, NEG_BIG)
    c = c_ref[0]
    c_one = jnp.concatenate([c, jnp.ones_like(c)], axis=1)

    def heads(with_bias):
        for i in range(nh * ngroups):
            h, g = divmod(i, ngroups)
            rows = pl.ds(i * rg, rg)
            qrows = pl.ds(g * rg, rg)
            lg = _dot_nt(qlat_ref[0, h, qrows, :], c) + madd_ref[qrows, :]
            if with_bias:
                lg = lg + bias_ref[0, h, qrows, :]
            m_old = m_ref[rows, :]
            m_new = jnp.maximum(m_old, jnp.max(lg, axis=1, keepdims=True))
            alpha = jnp.exp(m_old - m_new)
            p = jnp.exp(lg - jnp.concatenate([m_new] * (tk // LANES), axis=1))
            m_ref[rows, :] = m_new
            acc_ref[rows, :] = (jnp.concatenate([alpha, alpha], axis=1) * acc_ref[rows, :]
                                + _dot(p.astype(BF16), c_one))

    @pl.when(qi - kb < 2)
    def _():
        heads(True)

    @pl.when(qi - kb >= 2)
    def _():
        heads(False)

    @pl.when(kb == qi)
    def _():
        for j in range(nh // 2):
            r0, r1 = pl.ds(2 * j * tq, tq), pl.ds((2 * j + 1) * tq, tq)
            o = jnp.concatenate([acc_ref[r0, :KV_LATENT] / acc_ref[r0, KV_LATENT:],
                                 acc_ref[r1, :KV_LATENT] / acc_ref[r1, KV_LATENT:]], axis=1)
            out_ref[0, :, j * LANES:(j + 1) * LANES] = _dot(o.astype(BF16), wuv_ref[j])


def _dsa_attend(qlat, c, mask, bias, wuv2, tq):
    b, nh, s, _ = qlat.shape
    nq = s // tq
    rg = min(128, tq)
    qi_tab = np.concatenate([np.full(q + 1, q, np.int32) for q in range(nq)])
    kb_tab = np.concatenate([np.arange(q + 1, dtype=np.int32) for q in range(nq)])
    grid_spec = pltpu.PrefetchScalarGridSpec(
        num_scalar_prefetch=2,
        grid=(b, len(qi_tab)),
        in_specs=[pl.BlockSpec((1, nh, tq, KV_LATENT), lambda i, p, qt, kt: (i, 0, qt[p], 0)),
                  pl.BlockSpec((1, tq, c.shape[2]), lambda i, p, qt, kt: (i, kt[p], 0)),
                  pl.BlockSpec((1, tq, mask.shape[2]), lambda i, p, qt, kt: (i, qt[p], 0)),
                  pl.BlockSpec((1, nh, tq, tq),
                               lambda i, p, qt, kt: (jnp.minimum(qt[p] - kt[p], 1), 0, 0, 0)),
                  pl.BlockSpec(wuv2.shape, lambda i, p, qt, kt: (0, 0, 0))],
        out_specs=pl.BlockSpec((1, tq, nh * HEAD_DIM), lambda i, p, qt, kt: (i, qt[p], 0)),
        scratch_shapes=[pltpu.VMEM((nh * tq, LANES), F32),
                        pltpu.VMEM((nh * tq, 2 * KV_LATENT), F32),
                        pltpu.VMEM((tq, tq), F32)],
    )
    return pl.pallas_call(
        functools.partial(_dsa_attend_kernel, nh=nh, rg=rg),
        grid_spec=grid_spec,
        out_shape=jax.ShapeDtypeStruct((b, s, nh * HEAD_DIM), F32),
        compiler_params=_params("arbitrary", "arbitrary"),
        name="dsa_attend",
    )(jnp.asarray(qi_tab), jnp.asarray(kb_tab), qlat, c, mask, bias, wuv2)


def _t5_bucket(rel):
    n = jnp.maximum(rel, 0)
    max_exact = REL_BUCKETS // 2
    large = max_exact + (jnp.log(jnp.maximum(n, 1).astype(F32) / max_exact)
                         / math.log(REL_MAX_DIST / max_exact) * (REL_BUCKETS - max_exact)).astype(I32)
    large = jnp.minimum(large, REL_BUCKETS - 1)
    return jnp.where(n < max_exact, n, large)


def _bias_tiles(rel_bias, tq):
    assert tq >= REL_MAX_DIST
    i = jnp.arange(tq, dtype=I32)
    rel0 = i[:, None] - i[None, :]
    last = rel_bias[REL_BUCKETS - 1]
    tabs = [rel_bias[_t5_bucket(rel0 + d * tq)] - last for d in range(2)]
    return jnp.stack([jnp.moveaxis(t, -1, 0) for t in tabs])


def _post_kernel(x_ref, tok_ref, mq_ref, kv_ref, wout_ref, g_ref, b_ref, x1_ref, *, tok_w):
    mq = mq_ref[0]
    kv = kv_ref[0]
    y = _dot(tok_ref[0].astype(BF16), wout_ref[:tok_w, :])
    for h in range(MEM_HEADS):
        q = mq[:, h * HEAD_DIM:(h + 1) * HEAD_DIM].astype(BF16)
        k = kv[:, h * HEAD_DIM:(h + 1) * HEAD_DIM]
        v = kv[:, MEM_WIDTH + h * HEAD_DIM:MEM_WIDTH + (h + 1) * HEAD_DIM]
        lg = _dot_nt(q, k) * HEAD_DIM ** -0.5
        p = jnp.exp(lg - jnp.max(lg, axis=1, keepdims=True))
        p = p / jnp.sum(p, axis=1, keepdims=True)
        o = _dot(p.astype(BF16), v)
        y = y + _dot(o.astype(BF16), wout_ref[tok_w + h * HEAD_DIM:tok_w + (h + 1) * HEAD_DIM, :])
    x1_ref[0] = _layer_norm(DN_ALPHA * x_ref[0] + y, g_ref[...], b_ref[...])


def _post_mixer(x, tok, mq, kv, w_out, g, b, ts):
    bsz, s, d = x.shape
    tok_w = tok.shape[2]
    const = lambda shape: pl.BlockSpec(shape, lambda i, j: (0,) * len(shape))
    tile = lambda w_: pl.BlockSpec((1, ts, w_), lambda i, j: (i, j, 0))
    return pl.pallas_call(
        functools.partial(_post_kernel, tok_w=tok_w),
        grid=(bsz, s // ts),
        in_specs=[tile(d), tile(tok_w), tile(mq.shape[2]),
                  pl.BlockSpec((1,) + kv.shape[1:], lambda i, j: (i, 0, 0)),
                  const(w_out.shape), const(g.shape), const(b.shape)],
        out_specs=tile(d),
        out_shape=jax.ShapeDtypeStruct((bsz, s, d), F32),
        compiler_params=_params("arbitrary", "arbitrary"),
        name="post_mixer",
    )(x, tok, mq, kv, w_out, g, b)


def _router_kernel(x_ref, wh_ref, wl_ref, rb_ref, e_ref, gate_ref, rank_ref, cnt_ref, carry_ref):
    @pl.when(pl.program_id(0) == 0)
    def _():
        carry_ref[...] = jnp.zeros_like(carry_ref)

    x = x_ref[...]
    tr = x.shape[0]
    xh = x.astype(BF16)
    xl = (x - xh.astype(F32)).astype(BF16)
    lg = _dot(xh, wh_ref[...]) + (_dot(xl, wh_ref[...]) + _dot(xh, wl_ref[...])) + rb_ref[...]
    lane = lax.broadcasted_iota(I32, lg.shape, 1).astype(F32)
    lg = jnp.where(lane < N_EXPERTS, lg, -jnp.inf)

    idx, val = [], []
    onehot = jnp.zeros(lg.shape, F32)
    for _ in range(TOP_K):
        m = jnp.max(lg, axis=1, keepdims=True)
        i = jnp.min(jnp.where(lg == m, lane, float(LANES)), axis=1, keepdims=True)
        hit = lane == i
        idx.append(i)
        val.append(m)
        onehot = jnp.where(hit, 1.0, onehot)
        lg = jnp.where(hit, -jnp.inf, lg)

    ex = [jnp.exp(v - val[0]) for v in val]
    den = ex[0] + ex[1] + ex[2] + ex[3]

    r_i = lax.broadcasted_iota(I32, (tr, tr), 0)
    c_i = lax.broadcasted_iota(I32, (tr, tr), 1)
    tri = jnp.where(c_i < r_i, 1.0, 0.0).astype(BF16)
    before = _dot(tri, onehot.astype(BF16)) + carry_ref[...]
    carry_ref[...] = carry_ref[...] + jnp.sum(onehot, axis=0, keepdims=True)
    cnt_ref[...] = carry_ref[...]

    lane_k = lax.broadcasted_iota(I32, (tr, TOP_K), 1)
    e_out = jnp.zeros((tr, TOP_K), I32)
    g_out = jnp.zeros((tr, TOP_K), F32)
    r_out = jnp.zeros((tr, TOP_K), I32)
    for k in range(TOP_K):
        rk = jnp.sum(jnp.where(lane == idx[k], before, 0.0), axis=1, keepdims=True).astype(I32)
        e_out = jnp.where(lane_k == k, idx[k].astype(I32), e_out)
        g_out = jnp.where(lane_k == k, ex[k] / den, g_out)
        r_out = jnp.where(lane_k == k, rk, r_out)
    e_ref[...] = e_out
    gate_ref[...] = g_out
    rank_ref[...] = r_out


def _router(x, wh, wl, rb, tr):
    t, d = x.shape
    const = lambda shape: pl.BlockSpec(shape, lambda i: (0,) * len(shape))
    small = pl.BlockSpec((tr, TOP_K), lambda i: (i, 0))
    return pl.pallas_call(
        _router_kernel,
        grid=(t // tr,),
        in_specs=[pl.BlockSpec((tr, d), lambda i: (i, 0)), const(wh.shape), const(wl.shape),
                  const(rb.shape)],
        out_specs=[small, small, small, const((1, LANES))],
        out_shape=[jax.ShapeDtypeStruct((t, TOP_K), I32), jax.ShapeDtypeStruct((t, TOP_K), F32),
                   jax.ShapeDtypeStruct((t, TOP_K), I32), jax.ShapeDtypeStruct((1, LANES), F32)],
        scratch_shapes=[pltpu.VMEM((1, LANES), F32)],
        compiler_params=_params("arbitrary"),
        name="router",
    )(x, wh, wl, rb)


def _dest_kernel(e_ref, rank_ref, start_ref, dest_ref):
    e = e_ref[...]
    lane = lax.broadcasted_iota(I32, (e.shape[0], LANES), 1)
    lane_k = lax.broadcasted_iota(I32, e.shape, 1)
    out = rank_ref[...]
    for k in range(TOP_K):
        st = jnp.sum(jnp.where(lane == e[:, k:k + 1], start_ref[...], 0.0), axis=1, keepdims=True)
        out = out + jnp.where(lane_k == k, st.astype(I32), 0)
    dest_ref[...] = out


def _dest(e, rank, start, tr):
    t = e.shape[0]
    small = pl.BlockSpec((tr, TOP_K), lambda i: (i, 0))
    return pl.pallas_call(
        _dest_kernel,
        grid=(t // tr,),
        in_specs=[small, small, pl.BlockSpec((1, LANES), lambda i: (0, 0))],
        out_specs=small,
        out_shape=jax.ShapeDtypeStruct((t, TOP_K), I32),
        compiler_params=_params("arbitrary"),
        name="dest",
    )(e, rank, start)


def _dispatch_kernel(dest_ref, x_ref, xs_in_ref, xs_ref, sem):
    del xs_in_ref
    tt = x_ref.shape[0]

    def row_copy(i, k):
        d = dest_ref[i * TOP_K + k]
        return pltpu.make_async_copy(x_ref.at[pl.ds(i, 1)], xs_ref.at[pl.ds(d, 1)], sem)

    def issue(i, _):
        for k in range(TOP_K):
            row_copy(i, k).start()
        return 0

    def drain(i, _):
        for k in range(TOP_K):
            row_copy(i, k).wait()
        return 0

    lax.fori_loop(0, tt, issue, 0)
    lax.fori_loop(0, tt, drain, 0)


def _dispatch(dest_flat, x, rows, tt):
    t, d = x.shape
    xs0 = jnp.zeros((rows, d), x.dtype)
    return pl.pallas_call(
        _dispatch_kernel,
        grid=(t // tt,),
        in_specs=[pl.BlockSpec((tt * TOP_K,), lambda i: (i,), memory_space=pltpu.SMEM),
                  pl.BlockSpec((tt, d), lambda i: (i, 0)),
                  pl.BlockSpec(memory_space=pl.ANY)],
        out_specs=pl.BlockSpec(memory_space=pl.ANY),
        out_shape=jax.ShapeDtypeStruct((rows, d), x.dtype),
        scratch_shapes=[pltpu.SemaphoreType.DMA(())],
        input_output_aliases={2: 0},
        compiler_params=_params("arbitrary"),
        name="dispatch",
    )(dest_flat, x, xs0)


def _ffn_kernel(be_ref, xs_ref, w1_ref, b1_ref, w2_ref, b2_ref, ys_ref):
    del be_ref
    de = w2_ref.shape[1]
    h = _dot(xs_ref[...].astype(BF16), w1_ref[0]) + b1_ref[0]
    gt = jnp.minimum(h[:, :de], SWIGLU_LIMIT)
    up = jnp.clip(h[:, de:], -SWIGLU_LIMIT, SWIGLU_LIMIT)
    act = (up + 1.0) * (gt * _sigmoid(SWIGLU_ALPHA * gt))
    ys_ref[...] = _dot(act.astype(BF16), w2_ref[0]) + b2_ref[0]


def _expert_ffn(blk_e, xs, w1, b1, w2, b2, tm):
    rows, d = xs.shape
    grid_spec = pltpu.PrefetchScalarGridSpec(
        num_scalar_prefetch=1,
        grid=(rows // tm,),
        in_specs=[pl.BlockSpec((tm, d), lambda i, be: (i, 0)),
                  pl.BlockSpec((1,) + w1.shape[1:], lambda i, be: (be[i], 0, 0)),
                  pl.BlockSpec((1,) + b1.shape[1:], lambda i, be: (be[i], 0, 0)),
                  pl.BlockSpec((1,) + w2.shape[1:], lambda i, be: (be[i], 0, 0)),
                  pl.BlockSpec((1,) + b2.shape[1:], lambda i, be: (be[i], 0, 0))],
        out_specs=pl.BlockSpec((tm, d), lambda i, be: (i, 0)),
    )
    return pl.pallas_call(
        _ffn_kernel,
        grid_spec=grid_spec,
        out_shape=jax.ShapeDtypeStruct((rows, d), F32),
        compiler_params=_params("arbitrary"),
        name="expert_ffn",
    )(blk_e, xs, w1, b1, w2, b2)


def _combine_kernel(dest_ref, x_ref, gate_ref, g_ref, b_ref, ys_ref, out_ref, buf, sem):
    tt = x_ref.shape[0]

    def row_copy(i, k):
        d = dest_ref[i * TOP_K + k]
        return pltpu.make_async_copy(ys_ref.at[pl.ds(d, 1)], buf.at[k, pl.ds(i, 1)], sem)

    def issue(i, _):
        for k in range(TOP_K):
            row_copy(i, k).start()
        return 0

    def drain(i, _):
        for k in range(TOP_K):
            row_copy(i, k).wait()
        return 0

    lax.fori_loop(0, tt, issue, 0)
    lax.fori_loop(0, tt, drain, 0)

    gate = gate_ref[...]
    ffn = gate[:, 0:1] * buf[0]
    for k in range(1, TOP_K):
        ffn = ffn + gate[:, k:k + 1] * buf[k]
    out_ref[...] = _layer_norm(DN_ALPHA * x_ref[...] + ffn, g_ref[...], b_ref[...])


def _combine(dest_flat, x, gate, g, b, ys, tt):
    t, d = x.shape
    const = lambda shape: pl.BlockSpec(shape, lambda i: (0,) * len(shape))
    return pl.pallas_call(
        _combine_kernel,
        grid=(t // tt,),
        in_specs=[pl.BlockSpec((tt * TOP_K,), lambda i: (i,), memory_space=pltpu.SMEM),
                  pl.BlockSpec((tt, d), lambda i: (i, 0)),
                  pl.BlockSpec((tt, TOP_K), lambda i: (i, 0)),
                  const(g.shape), const(b.shape),
                  pl.BlockSpec(memory_space=pl.ANY)],
        out_specs=pl.BlockSpec((tt, d), lambda i: (i, 0)),
        out_shape=jax.ShapeDtypeStruct((t, d), F32),
        scratch_shapes=[pltpu.VMEM((TOP_K, tt, d), F32), pltpu.SemaphoreType.DMA(())],
        compiler_params=_params("arbitrary"),
        name="combine",
    )(dest_flat, x, gate, g, b, ys)


def _moe_layer(x, router_w, router_b, w1, b1, w2, b2, g, b, tiles):
    t, d = x.shape
    tm, tt, tr = tiles["tm"], tiles["tt"], tiles["tr"]
    wpad = jnp.pad(router_w, ((0, 0), (0, LANES - N_EXPERTS)))
    wh = wpad.astype(BF16)
    wl = (wpad - wh.astype(F32)).astype(BF16)
    rb = jnp.pad(router_b, (0, LANES - N_EXPERTS))[None, :]
    e, gate, rank, cnt = _router(x, wh, wl, rb, tr)

    counts = cnt[0, :N_EXPERTS].astype(I32)
    padded = (counts + tm - 1) // tm * tm
    pend = jnp.cumsum(padded)
    start = jnp.pad(pend - padded, (0, LANES - N_EXPERTS))[None, :].astype(F32)
    ntile = t * TOP_K // tm + N_EXPERTS
    blk_e = jnp.minimum(jnp.searchsorted(pend, jnp.arange(ntile, dtype=I32) * tm, side='right'),
                        N_EXPERTS - 1).astype(I32)

    dest = _dest(e, rank, start, tr).reshape(-1)
    xs = _dispatch(dest, x, ntile * tm, tt)
    ys = _expert_ffn(blk_e, xs, w1.astype(BF16), b1[:, None, :], w2.astype(BF16), b2[:, None, :], tm)
    return _combine(dest, x, gate, g[None, :], b[None, :], ys, tt)


def _block_diag(w):
    n, c, _ = w.shape
    eye = jnp.eye(n, dtype=w.dtype)
    return (eye[:, None, :, None] * w[:, :, None, :]).reshape(n * c, n * c)


def kernel(x, mem, rel_bias, a_w_in, a_conv_w, a_conv_b, a_wr, a_br, a_wi, a_bi, a_lambda, b_w_in, b_kv_norm_g, b_w_uk, b_w_uv, b_idx_norm_g, b_idx_norm_b, w_mem_kv, w_out, ln1_g, ln1_b, router_w, router_b, exp_w1, exp_b1, exp_w2, exp_b2, ln2_g, ln2_b):
    bsz, seq, d = x.shape
    tiles = _tiles(seq)
    ts = tiles["ts"]
    row = lambda v: v[None, :]

    def finish_layer(layer, x, tok, mq):
        kv = _matmul(mem.reshape(-1, d), w_mem_kv[layer].astype(BF16), mem.shape[1])
        kv = kv.astype(BF16).reshape(bsz, mem.shape[1], -1)
        x1 = _post_mixer(x, tok, mq, kv, w_out[layer].astype(BF16), row(ln1_g[layer]),
                         row(ln1_b[layer]), ts)
        x2 = _moe_layer(x1.reshape(-1, d), router_w[layer], router_b[layer], exp_w1[layer],
                        exp_b1[layer], exp_w2[layer], exp_b2[layer], ln2_g[layer], ln2_b[layer],
                        tiles)
        return x2.reshape(bsz, seq, d)

    tok, mq = _rglru_front(x, a_w_in[0].astype(BF16), a_conv_w[0], row(a_conv_b[0]),
                           _block_diag(a_wr[0]).astype(BF16), row(a_br[0]),
                           _block_diag(a_wi[0]).astype(BF16), row(a_bi[0]), row(a_lambda[0]), ts)
    x = finish_layer(0, x, tok, mq)

    nh = b_w_uk.shape[2]
    tok_w = nh * HEAD_DIM
    q0, c0, iq0, mq0, ik0, iw0, total = _dsa_layout(tok_w)
    w = b_w_in[0]
    o_c, o_iq = tok_w, tok_w + KV_LATENT
    o_ik = o_iq + IDX_HEADS * IDX_DIM
    o_iw = o_ik + IDX_DIM
    o_mq = o_iw + IDX_HEADS
    w_re = jnp.zeros((d, total), F32)
    w_re = w_re.at[:, q0:q0 + tok_w].set(w[:, :tok_w])
    w_re = w_re.at[:, c0:c0 + KV_LATENT].set(w[:, o_c:o_iq])
    w_re = w_re.at[:, iq0:iq0 + IDX_HEADS * IDX_DIM].set(w[:, o_iq:o_ik])
    w_re = w_re.at[:, ik0:ik0 + IDX_DIM].set(w[:, o_ik:o_iw])
    w_re = w_re.at[:, iw0:iw0 + IDX_HEADS].set(w[:, o_iw:o_mq])
    w_re = w_re.at[:, mq0:mq0 + MEM_WIDTH].set(w[:, o_mq:])
    wuk = jnp.transpose(b_w_uk[0], (1, 2, 0)).astype(BF16)
    wuv = jnp.transpose(b_w_uv[0], (1, 0, 2))
    wuv2 = jnp.zeros((nh // 2, 2 * KV_LATENT, 2 * HEAD_DIM), F32)
    wuv2 = wuv2.at[:, :KV_LATENT, :HEAD_DIM].set(wuv[0::2])
    wuv2 = wuv2.at[:, KV_LATENT:, HEAD_DIM:].set(wuv[1::2]).astype(BF16)
    qlat, c, iq, ik, iw, mq = _dsa_proj(x, w_re.astype(BF16), wuk, row(b_kv_norm_g[0]),
                                        row(b_idx_norm_g[0]), row(b_idx_norm_b[0]), ts)
    k_sel = min(TOPK_MAX, seq // 4)
    chunk = tiles["chunk"]
    ikt = jnp.swapaxes(ik.reshape(bsz, MASK_BITS, chunk, IDX_DIM), 2, 3)
    mask = _dsa_select(iq, iw, ikt, chunk, k_sel)
    tok = _dsa_attend(qlat, c, mask, _bias_tiles(rel_bias, tiles["tq_att"]), wuv2, tiles["tq_att"])
    x = finish_layer(1, x, tok, mq)
    return x
```

```python
import functools
import math

import jax
import jax.numpy as jnp
import numpy as np
from jax import lax
from jax.experimental import pallas as pl
from jax.experimental.pallas import tpu as pltpu

F32 = jnp.float32
BF16 = jnp.bfloat16
I32 = jnp.int32

HEAD_DIM = 64
MEM_HEADS = 4
MEM_WIDTH = MEM_HEADS * HEAD_DIM
LRU_C = 8.0
CONV_W = 4
KV_LATENT = 128
IDX_HEADS = 4
IDX_DIM = 64
TOPK_MAX = 256
REL_BUCKETS = 32
REL_MAX_DIST = 128
N_EXPERTS = 32
TOP_K = 4
SWIGLU_LIMIT = 7.0
SWIGLU_ALPHA = 1.702
DEPTH = 2
DN_ALPHA = (2 * DEPTH) ** 0.25
LN_EPS = 1e-5
RMS_EPS = 1e-6

SUBLANES = 8
LANES = 128
MASK_BITS = 32
INT_MIN = -2 ** 31
NEG_BIG = -1e30
VMEM_LIMIT = 56 * 1024 * 1024


def _tiles(seq):
    chunk = seq // MASK_BITS
    return dict(
        ts=min(512, seq),
        chunk=chunk,
        tq_att=chunk,
        tt=256,
        tm=256,
        tr=512,
    )


def _dot(a, b):
    return jnp.dot(a, b, preferred_element_type=F32)


def _dot_nt(a, b):
    return lax.dot_general(a, b, (((1,), (1,)), ((), ())), preferred_element_type=F32)


def _sigmoid(x):
    return 1.0 / (1.0 + jnp.exp(-x))


def _layer_norm(z, g, b):
    mu = jnp.mean(z, axis=-1, keepdims=True)
    zc = z - mu
    var = jnp.mean(zc * zc, axis=-1, keepdims=True)
    return zc * lax.rsqrt(var + LN_EPS) * g + b


def _params(*sem):
    return pltpu.CompilerParams(dimension_semantics=sem, vmem_limit_bytes=VMEM_LIMIT)


def _mm_kernel(a_ref, b_ref, o_ref):
    o_ref[...] = _dot(a_ref[...].astype(BF16), b_ref[...])


def _matmul(a, b, tm):
    m, k = a.shape
    n = b.shape[1]
    return pl.pallas_call(
        _mm_kernel,
        grid=(m // tm,),
        in_specs=[pl.BlockSpec((tm, k), lambda i: (i, 0)),
                  pl.BlockSpec((k, n), lambda i: (0, 0))],
        out_specs=pl.BlockSpec((tm, n), lambda i: (i, 0)),
        out_shape=jax.ShapeDtypeStruct((m, n), F32),
        compiler_params=_params("arbitrary"),
        name="matmul",
    )(a, b)


def _rglru_kernel(x_ref, win_ref, cw_ref, cb_ref, wr_ref, br_ref, wi_ref, bi_ref, lam_ref,
                  tok_ref, mq_ref, tail_ref, h_ref, a_s, u_s, *, tok_w):
    s = pl.program_id(1)

    @pl.when(s == 0)
    def _():
        tail_ref[...] = jnp.zeros_like(tail_ref)
        h_ref[...] = jnp.zeros_like(h_ref)

    proj = _dot(x_ref[0].astype(BF16), win_ref[...])
    xb = proj[:, :tok_w]
    gb = proj[:, tok_w:2 * tok_w]
    mq_ref[0] = proj[:, 2 * tok_w:]
    ts = xb.shape[0]

    xe = jnp.concatenate([tail_ref[...], xb], axis=0)
    tail_ref[...] = xb[ts - SUBLANES:, :]
    xc = xb * cw_ref[CONV_W - 1:CONV_W, :] + cb_ref[...]
    for d in range(1, CONV_W):
        sh = pltpu.roll(xe, d, 0)[SUBLANES:, :]
        xc = xc + sh * cw_ref[CONV_W - 1 - d:CONV_W - d, :]

    xcb = xc.astype(BF16)
    r = _sigmoid(_dot(xcb, wr_ref[...]) + br_ref[...])
    ig = _sigmoid(_dot(xcb, wi_ref[...]) + bi_ref[...])
    nl = -lam_ref[...]
    softplus = jnp.maximum(nl, 0.0) + jnp.log1p(jnp.exp(-jnp.abs(nl)))
    log_a = (-LRU_C) * r * softplus
    a = jnp.exp(log_a)
    th = jnp.tanh(log_a)
    u = jnp.sqrt(-2.0 * th / (1.0 - th)) * (ig * xc)

    row = lax.broadcasted_iota(I32, a.shape, 0) & (SUBLANES - 1)
    for d in (1, 2, 4):
        m = row >= d
        u = jnp.where(m, a * pltpu.roll(u, d, 0) + u, u)
        a = jnp.where(m, a * pltpu.roll(a, d, 0), a)
    a_s[...] = a
    u_s[...] = u

    def group(g, h):
        sl = pl.ds(pl.multiple_of(g * SUBLANES, SUBLANES), SUBLANES)
        hg = u_s[sl, :] + a_s[sl, :] * h
        u_s[sl, :] = hg
        return jnp.broadcast_to(hg[SUBLANES - 1:, :], hg.shape)

    h_ref[...] = lax.fori_loop(0, ts // SUBLANES, group, h_ref[...])

    gelu = 0.5 * gb * (1.0 + jnp.tanh(math.sqrt(2.0 / math.pi) * (gb + 0.044715 * (gb * gb * gb))))
    tok_ref[0] = u_s[...] * gelu


def _rglru_front(x, w_in, conv_w, conv_b, wr, br, wi, bi, lam, ts):
    b, s, d = x.shape
    tok_w = conv_w.shape[-1]
    mem_w = w_in.shape[-1] - 2 * tok_w
    const = lambda shape: pl.BlockSpec(shape, lambda i, j: (0,) * len(shape))
    return pl.pallas_call(
        functools.partial(_rglru_kernel, tok_w=tok_w),
        grid=(b, s // ts),
        in_specs=[pl.BlockSpec((1, ts, d), lambda i, j: (i, j, 0)),
                  const(w_in.shape), const(conv_w.shape), const(conv_b.shape),
                  const(wr.shape), const(br.shape), const(wi.shape), const(bi.shape),
                  const(lam.shape)],
        out_specs=[pl.BlockSpec((1, ts, tok_w), lambda i, j: (i, j, 0)),
                   pl.BlockSpec((1, ts, mem_w), lambda i, j: (i, j, 0))],
        out_shape=[jax.ShapeDtypeStruct((b, s, tok_w), F32),
                   jax.ShapeDtypeStruct((b, s, mem_w), F32)],
        scratch_shapes=[pltpu.VMEM((SUBLANES, tok_w), F32), pltpu.VMEM((SUBLANES, tok_w), F32),
                        pltpu.VMEM((ts, tok_w), F32), pltpu.VMEM((ts, tok_w), F32)],
        compiler_params=_params("arbitrary", "arbitrary"),
        name="rglru_front",
    )(x, w_in, conv_w, conv_b, wr, br, wi, bi, lam)


def _dsa_layout(tok_w):
    q0 = 0
    c0 = q0 + tok_w
    iq0 = c0 + KV_LATENT
    mq0 = iq0 + IDX_HEADS * IDX_DIM
    ik0 = mq0 + MEM_WIDTH
    iw0 = ik0 + LANES
    total = iw0 + LANES
    return q0, c0, iq0, mq0, ik0, iw0, total


def _dsa_proj_kernel(x_ref, w_ref, wuk_ref, kvg_ref, ig_ref, ib_ref,
                     qlat_ref, c_ref, iq_ref, ik_ref, iw_ref, mq_ref, *, tok_w):
    q0, c0, iq0, mq0, ik0, iw0, _ = _dsa_layout(tok_w)
    proj = _dot(x_ref[0].astype(BF16), w_ref[...])
    for h in range(tok_w // HEAD_DIM):
        qh = proj[:, q0 + h * HEAD_DIM:q0 + (h + 1) * HEAD_DIM].astype(BF16)
        ql = _dot(qh, wuk_ref[h]) * HEAD_DIM ** -0.5
        qlat_ref[0, h] = ql.astype(BF16)
    c = proj[:, c0:c0 + KV_LATENT]
    c = c * lax.rsqrt(jnp.mean(c * c, axis=-1, keepdims=True) + RMS_EPS) * kvg_ref[...]
    c_ref[0] = c.astype(BF16)
    iq_ref[0] = proj[:, iq0:iq0 + IDX_HEADS * IDX_DIM].astype(BF16)
    ik = _layer_norm(proj[:, ik0:ik0 + IDX_DIM], ig_ref[...], ib_ref[...])
    ik_ref[0] = ik.astype(BF16)
    iw_ref[0] = proj[:, iw0:iw0 + IDX_HEADS] * (IDX_HEADS ** -0.5 * IDX_DIM ** -0.5)
    mq_ref[0] = proj[:, mq0:mq0 + MEM_WIDTH]


def _dsa_proj(x, w, wuk, kvg, ig, ib, ts):
    b, s, d = x.shape
    nh = wuk.shape[0]
    tok_w = nh * HEAD_DIM
    const = lambda shape: pl.BlockSpec(shape, lambda i, j: (0,) * len(shape))
    tile = lambda w_: pl.BlockSpec((1, ts, w_), lambda i, j: (i, j, 0))
    widths = [KV_LATENT, IDX_HEADS * IDX_DIM, IDX_DIM, IDX_HEADS, MEM_WIDTH]
    dtypes = [BF16, BF16, BF16, F32, F32]
    return pl.pallas_call(
        functools.partial(_dsa_proj_kernel, tok_w=tok_w),
        grid=(b, s // ts),
        in_specs=[tile(d), const(w.shape), const(wuk.shape), const(kvg.shape),
                  const(ig.shape), const(ib.shape)],
        out_specs=[pl.BlockSpec((1, nh, ts, KV_LATENT), lambda i, j: (i, 0, j, 0))]
        + [tile(w_) for w_ in widths],
        out_shape=[jax.ShapeDtypeStruct((b, nh, s, KV_LATENT), BF16)]
        + [jax.ShapeDtypeStruct((b, s, w_), t) for w_, t in zip(widths, dtypes)],
        compiler_params=_params("arbitrary", "arbitrary"),
        name="dsa_proj",
    )(x, w, wuk, kvg, ig, ib)


def _row_popcount(words):
    pc = lax.population_count(words)
    part = pc[:, :LANES]
    for i in range(1, words.shape[1] // LANES):
        part = part + pc[:, i * LANES:(i + 1) * LANES]
    return jnp.sum(part.astype(F32), axis=1, keepdims=True)


def _dsa_select_kernel(iq_ref, iw_ref, ikt_ref, mask_ref, keys_ref, iwb_ref, tie_ref,
                       *, k_sel, chunk):
    qi = pl.program_id(1)
    tq = iq_ref.shape[1]
    iq = iq_ref[0]
    iw = iw_ref[0]
    for h in range(IDX_HEADS):
        iwb_ref[h] = jnp.broadcast_to(iw[:, h:h + 1], (tq, chunk))
    iqh = [iq[:, h * IDX_DIM:(h + 1) * IDX_DIM] for h in range(IDX_HEADS)]

    def ukey(c):
        ikc = ikt_ref[0, c]
        sc = jnp.maximum(_dot(iqh[0], ikc), 0.0) * iwb_ref[0]
        for h in range(1, IDX_HEADS):
            sc = sc + jnp.maximum(_dot(iqh[h], ikc), 0.0) * iwb_ref[h]
        sc = jnp.where(sc == 0.0, 0.0, sc)
        bits = lax.bitcast_convert_type(sc, I32)
        return jnp.where(bits < 0, ~bits, bits ^ INT_MIN)

    def score(c, _):
        keys_ref[c] = ukey(c)
        return 0

    lax.fori_loop(0, qi, score, 0)
    row = lax.broadcasted_iota(I32, (tq, chunk), 0)
    lane = lax.broadcasted_iota(I32, (tq, chunk), 1)
    keys_ref[qi] = jnp.where(lane <= row, ukey(qi), 0)

    def clear(c, _):
        keys_ref[c] = jnp.zeros((tq, chunk), I32)
        return 0

    lax.fori_loop(qi + 1, MASK_BITS, clear, 0)

    def transpose(g, _):
        rows = pl.ds(pl.multiple_of(g * SUBLANES, SUBLANES), SUBLANES)
        for half in range(chunk // LANES):
            cols = slice(half * LANES, (half + 1) * LANES)
            a = [keys_ref[c, rows, cols] for c in range(MASK_BITS)]
            j, m = MASK_BITS // 2, 0x0000FFFF
            while j:
                k = 0
                while k < MASK_BITS:
                    t = (lax.shift_right_logical(a[k], j) ^ a[k + j]) & m
                    a[k] = a[k] ^ (t << j)
                    a[k + j] = a[k + j] ^ t
                    k = (k + j + 1) & ~j
                j >>= 1
                m = (m ^ (m << j)) & 0xFFFFFFFF
                m = m - (1 << 32) if m >= (1 << 31) else m
            for c in range(MASK_BITS):
                keys_ref[c, rows, cols] = a[c]
        return 0

    lax.fori_loop(0, tq // SUBLANES, transpose, 0)

    def plane_step(i, carry):
        active, greater, need, any_one = carry
        ones = active & keys_ref[MASK_BITS - 1 - i]
        cnt = _row_popcount(ones)
        take = cnt >= need
        active = jnp.where(take, ones, active & ~keys_ref[MASK_BITS - 1 - i])
        greater = jnp.where(take, greater, greater | ones)
        need = jnp.where(take, need, need - cnt)
        return active, greater, need, jnp.where(take, 1.0, any_one)

    full = jnp.full((tq, chunk), -1, I32)
    ties, greater, need, any_one = lax.fori_loop(
        0, MASK_BITS, plane_step,
        (full, jnp.zeros((tq, chunk), I32), jnp.full((tq, 1), float(k_sel), F32),
         jnp.zeros((tq, 1), F32)))
    ties = jnp.where(any_one > 0.0, ties, 0)
    tie_ref[...] = ties

    @pl.when(jnp.max(_row_popcount(ties) - need) > 0.0)
    def _():
        shift = chunk.bit_length() - 1
        lane_i = lax.broadcasted_iota(I32, (tq, chunk), 1)

        def below(v, inclusive):
            vh = jnp.broadcast_to(v >> shift, (tq, chunk))
            vl = jnp.broadcast_to(v & (chunk - 1), (tq, chunk))
            top = jnp.int32(1) << vh
            edge = (lane_i <= vl) if inclusive else (lane_i < vl)
            return (top - 1) | jnp.where(edge, top, 0)

        nbits = (MASK_BITS * chunk).bit_length() - 1

        def idx_step(i, j):
            cand = j | (jnp.int32(1) << (nbits - 1 - i))
            cnt = _row_popcount(ties & below(cand, False))
            return jnp.where(cnt < need, cand, j)

        last = lax.fori_loop(0, nbits, idx_step, jnp.zeros((tq, 1), I32))
        tie_ref[...] = ties & below(last, True)

    mask_ref[0] = greater | tie_ref[...]


def _dsa_select(iq, iw, ikt, chunk, k_sel):
    b, s, _ = iq.shape
    tq = chunk
    return pl.pallas_call(
        functools.partial(_dsa_select_kernel, k_sel=k_sel, chunk=chunk),
        grid=(b, s // tq),
        in_specs=[pl.BlockSpec((1, tq, iq.shape[2]), lambda i, j: (i, j, 0)),
                  pl.BlockSpec((1, tq, iw.shape[2]), lambda i, j: (i, j, 0)),
                  pl.BlockSpec((1,) + ikt.shape[1:], lambda i, j: (i, 0, 0, 0))],
        out_specs=pl.BlockSpec((1, tq, chunk), lambda i, j: (i, j, 0)),
        out_shape=jax.ShapeDtypeStruct((b, s, chunk), I32),
        scratch_shapes=[pltpu.VMEM((MASK_BITS, tq, chunk), I32),
                        pltpu.VMEM((IDX_HEADS, tq, chunk), F32),
                        pltpu.VMEM((tq, chunk), I32)],
        compiler_params=_params("arbitrary", "arbitrary"),
        name="dsa_select",
    )(iq, iw, ikt)


def _dsa_attend_kernel(qi_ref, kb_ref, qlat_ref, c_ref, mask_ref, bias_ref, wuv_ref, out_ref,
                       m_ref, acc_ref, madd_ref, *, nh, rg):
    p_id = pl.program_id(1)
    qi = qi_ref[p_id]
    kb = kb_ref[p_id]
    tq, tk = madd_ref.shape
    ngroups = tq // rg

    @pl.when(kb == 0)
    def _():
        m_ref[...] = jnp.full(m_ref.shape, NEG_BIG, F32)
        acc_ref[...] = jnp.zeros_like(acc_ref)

    madd_ref[...] = jnp.where(((mask_ref[0] >> kb) & 1) == 1, 0.0, NEG_BIG)
    c = c_ref[0]
    c_one = jnp.concatenate([c, jnp.ones_like(c)], axis=1)

    def heads(with_bias):
        for i in range(nh * ngroups):
            h, g = divmod(i, ngroups)
            rows = pl.ds(i * rg, rg)
            qrows = pl.ds(g * rg, rg)
            lg = _dot_nt(qlat_ref[0, h, qrows, :], c) + madd_ref[qrows, :]
            if with_bias:
                lg = lg + bias_ref[0, h, qrows, :]
            m_old = m_ref[rows, :]
            m_new = jnp.maximum(m_old, jnp.max(lg, axis=1, keepdims=True))
            alpha = jnp.exp(m_old - m_new)
            p = jnp.exp(lg - jnp.concatenate([m_new] * (tk // LANES), axis=1))
            m_ref[rows, :] = m_new
            acc_ref[rows, :] = (jnp.concatenate([alpha, alpha], axis=1) * acc_ref[rows, :]
                                + _dot(p.astype(BF16), c_one))

    @pl.when(qi - kb < 2)
    def _():
        heads(True)

    @pl.when(qi - kb >= 2)
    def _():
        heads(False)

    @pl.when(kb == qi)
    def _():
        for j in range(nh // 2):
            r0, r1 = pl.ds(2 * j * tq, tq), pl.ds((2 * j + 1) * tq, tq)
            o = jnp.concatenate([acc_ref[r0, :KV_LATENT] / acc_ref[r0, KV_LATENT:],
                                 acc_ref[r1, :KV_LATENT] / acc_ref[r1, KV_LATENT:]], axis=1)
            out_ref[0, :, j * LANES:(j + 1) * LANES] = _dot(o.astype(BF16), wuv_ref[j])


def _dsa_attend(qlat, c, mask, bias, wuv2, tq):
    b, nh, s, _ = qlat.shape
    nq = s // tq
    rg = min(128, tq)
    qi_tab = np.concatenate([np.full(q + 1, q, np.int32) for q in range(nq)])
    kb_tab = np.concatenate([np.arange(q + 1, dtype=np.int32) for q in range(nq)])
    grid_spec = pltpu.PrefetchScalarGridSpec(
        num_scalar_prefetch=2,
        grid=(b, len(qi_tab)),
        in_specs=[pl.BlockSpec((1, nh, tq, KV_LATENT), lambda i, p, qt, kt: (i, 0, qt[p], 0)),
                  pl.BlockSpec((1, tq, c.shape[2]), lambda i, p, qt, kt: (i, kt[p], 0)),
                  pl.BlockSpec((1, tq, mask.shape[2]), lambda i, p, qt, kt: (i, qt[p], 0)),
                  pl.BlockSpec((1, nh, tq, tq),
                               lambda i, p, qt, kt: (jnp.minimum(qt[p] - kt[p], 1), 0, 0, 0)),
                  pl.BlockSpec(wuv2.shape, lambda i, p, qt, kt: (0, 0, 0))],
        out_specs=pl.BlockSpec((1, tq, nh * HEAD_DIM), lambda i, p, qt, kt: (i, qt[p], 0)),
        scratch_shapes=[pltpu.VMEM((nh * tq, LANES), F32),
                        pltpu.VMEM((nh * tq, 2 * KV_LATENT), F32),
                        pltpu.VMEM((tq, tq), F32)],
    )
    return pl.pallas_call(
        functools.partial(_dsa_attend_kernel, nh=nh, rg=rg),
        grid_spec=grid_spec,
        out_shape=jax.ShapeDtypeStruct((b, s, nh * HEAD_DIM), F32),
        compiler_params=_params("arbitrary", "arbitrary"),
        name="dsa_attend",
    )(jnp.asarray(qi_tab), jnp.asarray(kb_tab), qlat, c, mask, bias, wuv2)


def _t5_bucket(rel):
    n = jnp.maximum(rel, 0)
    max_exact = REL_BUCKETS // 2
    large = max_exact + (jnp.log(jnp.maximum(n, 1).astype(F32) / max_exact)
                         / math.log(REL_MAX_DIST / max_exact) * (REL_BUCKETS - max_exact)).astype(I32)
    large = jnp.minimum(large, REL_BUCKETS - 1)
    return jnp.where(n < max_exact, n, large)


def _bias_tiles(rel_bias, tq):
    assert tq >= REL_MAX_DIST
    i = jnp.arange(tq, dtype=I32)
    rel0 = i[:, None] - i[None, :]
    shifted = rel_bias - rel_bias[REL_BUCKETS - 1]
    buckets = jnp.stack([_t5_bucket(rel0 + d * tq) for d in range(2)])
    onehot = jax.nn.one_hot(buckets, REL_BUCKETS, dtype=F32)
    return jnp.einsum('dijb,bh->dhij', onehot, shifted, precision=lax.Precision.HIGHEST)


def _post_kernel(x_ref, tok_ref, mq_ref, kv_ref, wout_ref, g_ref, b_ref, x1_ref, *, tok_w):
    mq = mq_ref[0]
    kv = kv_ref[0]
    y = _dot(tok_ref[0].astype(BF16), wout_ref[:tok_w, :])
    for h in range(MEM_HEADS):
        q = mq[:, h * HEAD_DIM:(h + 1) * HEAD_DIM].astype(BF16)
        k = kv[:, h * HEAD_DIM:(h + 1) * HEAD_DIM]
        v = kv[:, MEM_WIDTH + h * HEAD_DIM:MEM_WIDTH + (h + 1) * HEAD_DIM]
        lg = _dot_nt(q, k) * HEAD_DIM ** -0.5
        p = jnp.exp(lg - jnp.max(lg, axis=1, keepdims=True))
        p = p / jnp.sum(p, axis=1, keepdims=True)
        o = _dot(p.astype(BF16), v)
        y = y + _dot(o.astype(BF16), wout_ref[tok_w + h * HEAD_DIM:tok_w + (h + 1) * HEAD_DIM, :])
    x1_ref[0] = _layer_norm(DN_ALPHA * x_ref[0] + y, g_ref[...], b_ref[...])


def _post_mixer(x, tok, mq, kv, w_out, g, b, ts):
    bsz, s, d = x.shape
    tok_w = tok.shape[2]
    const = lambda shape: pl.BlockSpec(shape, lambda i, j: (0,) * len(shape))
    tile = lambda w_: pl.BlockSpec((1, ts, w_), lambda i, j: (i, j, 0))
    return pl.pallas_call(
        functools.partial(_post_kernel, tok_w=tok_w),
        grid=(bsz, s // ts),
        in_specs=[tile(d), tile(tok_w), tile(mq.shape[2]),
                  pl.BlockSpec((1,) + kv.shape[1:], lambda i, j: (i, 0, 0)),
                  const(w_out.shape), const(g.shape), const(b.shape)],
        out_specs=tile(d),
        out_shape=jax.ShapeDtypeStruct((bsz, s, d), F32),
        compiler_params=_params("arbitrary", "arbitrary"),
        name="post_mixer",
    )(x, tok, mq, kv, w_out, g, b)


def _router_kernel(x_ref, wh_ref, wl_ref, rb_ref, e_ref, gate_ref, rank_ref, cnt_ref, carry_ref):
    @pl.when(pl.program_id(0) == 0)
    def _():
        carry_ref[...] = jnp.zeros_like(carry_ref)

    x = x_ref[...]
    tr = x.shape[0]
    xh = x.astype(BF16)
    xl = (x - xh.astype(F32)).astype(BF16)
    lg = _dot(xh, wh_ref[...]) + (_dot(xl, wh_ref[...]) + _dot(xh, wl_ref[...])) + rb_ref[...]
    lane = lax.broadcasted_iota(I32, lg.shape, 1).astype(F32)
    lg = jnp.where(lane < N_EXPERTS, lg, -jnp.inf)

    idx, val = [], []
    onehot = jnp.zeros(lg.shape, F32)
    for _ in range(TOP_K):
        m = jnp.max(lg, axis=1, keepdims=True)
        i = jnp.min(jnp.where(lg == m, lane, float(LANES)), axis=1, keepdims=True)
        hit = lane == i
        idx.append(i)
        val.append(m)
        onehot = jnp.where(hit, 1.0, onehot)
        lg = jnp.where(hit, -jnp.inf, lg)

    ex = [jnp.exp(v - val[0]) for v in val]
    den = ex[0] + ex[1] + ex[2] + ex[3]

    r_i = lax.broadcasted_iota(I32, (tr, tr), 0)
    c_i = lax.broadcasted_iota(I32, (tr, tr), 1)
    tri = jnp.where(c_i < r_i, 1.0, 0.0).astype(BF16)
    before = _dot(tri, onehot.astype(BF16)) + carry_ref[...]
    carry_ref[...] = carry_ref[...] + jnp.sum(onehot, axis=0, keepdims=True)
    cnt_ref[...] = carry_ref[...]

    lane_k = lax.broadcasted_iota(I32, (tr, TOP_K), 1)
    e_out = jnp.zeros((tr, TOP_K), I32)
    g_out = jnp.zeros((tr, TOP_K), F32)
    r_out = jnp.zeros((tr, TOP_K), I32)
    for k in range(TOP_K):
        rk = jnp.sum(jnp.where(lane == idx[k], before, 0.0), axis=1, keepdims=True).astype(I32)
        e_out = jnp.where(lane_k == k, idx[k].astype(I32), e_out)
        g_out = jnp.where(lane_k == k, ex[k] / den, g_out)
        r_out = jnp.where(lane_k == k, rk, r_out)
    e_ref[...] = e_out
    gate_ref[...] = g_out
    rank_ref[...] = r_out


def _router(x, wh, wl, rb, tr):
    t, d = x.shape
    const = lambda shape: pl.BlockSpec(shape, lambda i: (0,) * len(shape))
    small = pl.BlockSpec((tr, TOP_K), lambda i: (i, 0))
    return pl.pallas_call(
        _router_kernel,
        grid=(t // tr,),
        in_specs=[pl.BlockSpec((tr, d), lambda i: (i, 0)), const(wh.shape), const(wl.shape),
                  const(rb.shape)],
        out_specs=[small, small, small, const((1, LANES))],
        out_shape=[jax.ShapeDtypeStruct((t, TOP_K), I32), jax.ShapeDtypeStruct((t, TOP_K), F32),
                   jax.ShapeDtypeStruct((t, TOP_K), I32), jax.ShapeDtypeStruct((1, LANES), F32)],
        scratch_shapes=[pltpu.VMEM((1, LANES), F32)],
        compiler_params=_params("arbitrary"),
        name="router",
    )(x, wh, wl, rb)


def _dest_kernel(e_ref, rank_ref, start_ref, dest_ref):
    e = e_ref[...]
    lane = lax.broadcasted_iota(I32, (e.shape[0], LANES), 1)
    lane_k = lax.broadcasted_iota(I32, e.shape, 1)
    out = rank_ref[...]
    for k in range(TOP_K):
        st = jnp.sum(jnp.where(lane == e[:, k:k + 1], start_ref[...], 0.0), axis=1, keepdims=True)
        out = out + jnp.where(lane_k == k, st.astype(I32), 0)
    dest_ref[...] = out


def _dest(e, rank, start, tr):
    t = e.shape[0]
    small = pl.BlockSpec((tr, TOP_K), lambda i: (i, 0))
    return pl.pallas_call(
        _dest_kernel,
        grid=(t // tr,),
        in_specs=[small, small, pl.BlockSpec((1, LANES), lambda i: (0, 0))],
        out_specs=small,
        out_shape=jax.ShapeDtypeStruct((t, TOP_K), I32),
        compiler_params=_params("arbitrary"),
        name="dest",
    )(e, rank, start)


def _dispatch_kernel(dest_ref, x_ref, xs_in_ref, xs_ref, sem):
    del xs_in_ref
    tt = x_ref.shape[0]

    def row_copy(i, k):
        d = dest_ref[i * TOP_K + k]
        return pltpu.make_async_copy(x_ref.at[pl.ds(i, 1)], xs_ref.at[pl.ds(d, 1)], sem)

    def issue(i, _):
        for k in range(TOP_K):
            row_copy(i, k).start()
        return 0

    def drain(i, _):
        for k in range(TOP_K):
            row_copy(i, k).wait()
        return 0

    lax.fori_loop(0, tt, issue, 0, unroll=8)
    lax.fori_loop(0, tt, drain, 0, unroll=8)


def _dispatch(dest_flat, x, rows, tt):
    t, d = x.shape
    xs0 = jnp.zeros((rows, d), x.dtype)
    return pl.pallas_call(
        _dispatch_kernel,
        grid=(t // tt,),
        in_specs=[pl.BlockSpec((tt * TOP_K,), lambda i: (i,), memory_space=pltpu.SMEM),
                  pl.BlockSpec((tt, d), lambda i: (i, 0)),
                  pl.BlockSpec(memory_space=pl.ANY)],
        out_specs=pl.BlockSpec(memory_space=pl.ANY),
        out_shape=jax.ShapeDtypeStruct((rows, d), x.dtype),
        scratch_shapes=[pltpu.SemaphoreType.DMA(())],
        input_output_aliases={2: 0},
        compiler_params=_params("arbitrary"),
        name="dispatch",
    )(dest_flat, x, xs0)


def _ffn_kernel(be_ref, xs_ref, w1_ref, b1_ref, w2_ref, b2_ref, ys_ref):
    del be_ref
    de = w2_ref.shape[1]
    h = _dot(xs_ref[...].astype(BF16), w1_ref[0]) + b1_ref[0]
    gt = jnp.minimum(h[:, :de], SWIGLU_LIMIT)
    up = jnp.clip(h[:, de:], -SWIGLU_LIMIT, SWIGLU_LIMIT)
    act = (up + 1.0) * (gt * _sigmoid(SWIGLU_ALPHA * gt))
    ys_ref[...] = _dot(act.astype(BF16), w2_ref[0]) + b2_ref[0]


def _expert_ffn(blk_e, xs, w1, b1, w2, b2, tm):
    rows, d = xs.shape
    grid_spec = pltpu.PrefetchScalarGridSpec(
        num_scalar_prefetch=1,
        grid=(rows // tm,),
        in_specs=[pl.BlockSpec((tm, d), lambda i, be: (i, 0)),
                  pl.BlockSpec((1,) + w1.shape[1:], lambda i, be: (be[i], 0, 0)),
                  pl.BlockSpec((1,) + b1.shape[1:], lambda i, be: (be[i], 0, 0)),
                  pl.BlockSpec((1,) + w2.shape[1:], lambda i, be: (be[i], 0, 0)),
                  pl.BlockSpec((1,) + b2.shape[1:], lambda i, be: (be[i], 0, 0))],
        out_specs=pl.BlockSpec((tm, d), lambda i, be: (i, 0)),
    )
    return pl.pallas_call(
        _ffn_kernel,
        grid_spec=grid_spec,
        out_shape=jax.ShapeDtypeStruct((rows, d), F32),
        compiler_params=_params("arbitrary"),
        name="expert_ffn",
    )(blk_e, xs, w1, b1, w2, b2)


def _combine_kernel(dest_ref, x_ref, gate_ref, g_ref, b_ref, ys_ref, out_ref, buf, sem):
    tt = x_ref.shape[0]

    def row_copy(i, k):
        d = dest_ref[i * TOP_K + k]
        return pltpu.make_async_copy(ys_ref.at[pl.ds(d, 1)], buf.at[k, pl.ds(i, 1)], sem)

    def issue(i, _):
        for k in range(TOP_K):
            row_copy(i, k).start()
        return 0

    def drain(i, _):
        for k in range(TOP_K):
            row_copy(i, k).wait()
        return 0

    lax.fori_loop(0, tt, issue, 0, unroll=8)
    lax.fori_loop(0, tt, drain, 0, unroll=8)

    gate = gate_ref[...]
    ffn = gate[:, 0:1] * buf[0]
    for k in range(1, TOP_K):
        ffn = ffn + gate[:, k:k + 1] * buf[k]
    out_ref[...] = _layer_norm(DN_ALPHA * x_ref[...] + ffn, g_ref[...], b_ref[...])


def _combine(dest_flat, x, gate, g, b, ys, tt):
    t, d = x.shape
    const = lambda shape: pl.BlockSpec(shape, lambda i: (0,) * len(shape))
    return pl.pallas_call(
        _combine_kernel,
        grid=(t // tt,),
        in_specs=[pl.BlockSpec((tt * TOP_K,), lambda i: (i,), memory_space=pltpu.SMEM),
                  pl.BlockSpec((tt, d), lambda i: (i, 0)),
                  pl.BlockSpec((tt, TOP_K), lambda i: (i, 0)),
                  const(g.shape), const(b.shape),
                  pl.BlockSpec(memory_space=pl.ANY)],
        out_specs=pl.BlockSpec((tt, d), lambda i: (i, 0)),
        out_shape=jax.ShapeDtypeStruct((t, d), F32),
        scratch_shapes=[pltpu.VMEM((TOP_K, tt, d), F32), pltpu.SemaphoreType.DMA(())],
        compiler_params=_params("arbitrary"),
        name="combine",
    )(dest_flat, x, gate, g, b, ys)


def _moe_layer(x, router_w, router_b, w1, b1, w2, b2, g, b, tiles):
    t, d = x.shape
    tm, tt, tr = tiles["tm"], tiles["tt"], tiles["tr"]
    wpad = jnp.pad(router_w, ((0, 0), (0, LANES - N_EXPERTS)))
    wh = wpad.astype(BF16)
    wl = (wpad - wh.astype(F32)).astype(BF16)
    rb = jnp.pad(router_b, (0, LANES - N_EXPERTS))[None, :]
    e, gate, rank, cnt = _router(x, wh, wl, rb, tr)

    counts = cnt[0, :N_EXPERTS].astype(I32)
    padded = (counts + tm - 1) // tm * tm
    pend = jnp.cumsum(padded)
    start = jnp.pad(pend - padded, (0, LANES - N_EXPERTS))[None, :].astype(F32)
    ntile = t * TOP_K // tm + N_EXPERTS
    tile_row = jnp.arange(ntile, dtype=I32) * tm
    blk_e = jnp.minimum(jnp.sum(pend[None, :] <= tile_row[:, None], axis=1), N_EXPERTS - 1).astype(I32)

    dest = _dest(e, rank, start, tr).reshape(-1)
    xs = _dispatch(dest, x, ntile * tm, tt)
    ys = _expert_ffn(blk_e, xs, w1.astype(BF16), b1[:, None, :], w2.astype(BF16), b2[:, None, :], tm)
    return _combine(dest, x, gate, g[None, :], b[None, :], ys, tt)


def _block_diag(w):
    n, c, _ = w.shape
    eye = jnp.eye(n, dtype=w.dtype)
    return (eye[:, None, :, None] * w[:, :, None, :]).reshape(n * c, n * c)


def kernel(x, mem, rel_bias, a_w_in, a_conv_w, a_conv_b, a_wr, a_br, a_wi, a_bi, a_lambda, b_w_in, b_kv_norm_g, b_w_uk, b_w_uv, b_idx_norm_g, b_idx_norm_b, w_mem_kv, w_out, ln1_g, ln1_b, router_w, router_b, exp_w1, exp_b1, exp_w2, exp_b2, ln2_g, ln2_b):
    bsz, seq, d = x.shape
    tiles = _tiles(seq)
    ts = tiles["ts"]
    row = lambda v: v[None, :]

    def finish_layer(layer, x, tok, mq):
        kv = _matmul(mem.reshape(-1, d), w_mem_kv[layer].astype(BF16), mem.shape[1])
        kv = kv.astype(BF16).reshape(bsz, mem.shape[1], -1)
        x1 = _post_mixer(x, tok, mq, kv, w_out[layer].astype(BF16), row(ln1_g[layer]),
                         row(ln1_b[layer]), ts)
        x2 = _moe_layer(x1.reshape(-1, d), router_w[layer], router_b[layer], exp_w1[layer],
                        exp_b1[layer], exp_w2[layer], exp_b2[layer], ln2_g[layer], ln2_b[layer],
                        tiles)
        return x2.reshape(bsz, seq, d)

    tok, mq = _rglru_front(x, a_w_in[0].astype(BF16), a_conv_w[0], row(a_conv_b[0]),
                           _block_diag(a_wr[0]).astype(BF16), row(a_br[0]),
                           _block_diag(a_wi[0]).astype(BF16), row(a_bi[0]), row(a_lambda[0]), ts)
    x = finish_layer(0, x, tok, mq)

    nh = b_w_uk.shape[2]
    tok_w = nh * HEAD_DIM
    q0, c0, iq0, mq0, ik0, iw0, total = _dsa_layout(tok_w)
    w = b_w_in[0]
    o_c, o_iq = tok_w, tok_w + KV_LATENT
    o_ik = o_iq + IDX_HEADS * IDX_DIM
    o_iw = o_ik + IDX_DIM
    o_mq = o_iw + IDX_HEADS
    w_re = jnp.zeros((d, total), F32)
    w_re = w_re.at[:, q0:q0 + tok_w].set(w[:, :tok_w])
    w_re = w_re.at[:, c0:c0 + KV_LATENT].set(w[:, o_c:o_iq])
    w_re = w_re.at[:, iq0:iq0 + IDX_HEADS * IDX_DIM].set(w[:, o_iq:o_ik])
    w_re = w_re.at[:, ik0:ik0 + IDX_DIM].set(w[:, o_ik:o_iw])
    w_re = w_re.at[:, iw0:iw0 + IDX_HEADS].set(w[:, o_iw:o_mq])
    w_re = w_re.at[:, mq0:mq0 + MEM_WIDTH].set(w[:, o_mq:])
    wuk = jnp.transpose(b_w_uk[0], (1, 2, 0)).astype(BF16)
    wuv = jnp.transpose(b_w_uv[0], (1, 0, 2))
    wuv2 = jnp.zeros((nh // 2, 2 * KV_LATENT, 2 * HEAD_DIM), F32)
    wuv2 = wuv2.at[:, :KV_LATENT, :HEAD_DIM].set(wuv[0::2])
    wuv2 = wuv2.at[:, KV_LATENT:, HEAD_DIM:].set(wuv[1::2]).astype(BF16)
    qlat, c, iq, ik, iw, mq = _dsa_proj(x, w_re.astype(BF16), wuk, row(b_kv_norm_g[0]),
                                        row(b_idx_norm_g[0]), row(b_idx_norm_b[0]), ts)
    k_sel = min(TOPK_MAX, seq // 4)
    chunk = tiles["chunk"]
    ikt = jnp.swapaxes(ik.reshape(bsz, MASK_BITS, chunk, IDX_DIM), 2, 3)
    mask = _dsa_select(iq, iw, ikt, chunk, k_sel)
    tok = _dsa_attend(qlat, c, mask, _bias_tiles(rel_bias, tiles["tq_att"]), wuv2, tiles["tq_att"])
    x = finish_layer(1, x, tok, mq)
    return x
```

```python
import functools
import math

import jax
import jax.numpy as jnp
import numpy as np
from jax import lax
from jax.experimental import pallas as pl
from jax.experimental.pallas import tpu as pltpu

F32 = jnp.float32
BF16 = jnp.bfloat16
I32 = jnp.int32

HEAD_DIM = 64
MEM_HEADS = 4
MEM_WIDTH = MEM_HEADS * HEAD_DIM
LRU_C = 8.0
CONV_W = 4
KV_LATENT = 128
IDX_HEADS = 4
IDX_DIM = 64
TOPK_MAX = 256
REL_BUCKETS = 32
REL_MAX_DIST = 128
N_EXPERTS = 32
TOP_K = 4
SWIGLU_LIMIT = 7.0
SWIGLU_ALPHA = 1.702
DEPTH = 2
DN_ALPHA = (2 * DEPTH) ** 0.25
LN_EPS = 1e-5
RMS_EPS = 1e-6

SUBLANES = 8
LANES = 128
MASK_BITS = 32
INT_MIN = -2 ** 31
NEG_BIG = -1e30
VMEM_LIMIT = 56 * 1024 * 1024


def _tiles(seq):
    chunk = seq // MASK_BITS
    return dict(
        ts=min(512, seq),
        chunk=chunk,
        tq_att=chunk,
        tt=256,
        tm=512,
        tr=512,
    )


def _dot(a, b):
    return jnp.dot(a, b, preferred_element_type=F32)


def _dot_nt(a, b):
    return lax.dot_general(a, b, (((1,), (1,)), ((), ())), preferred_element_type=F32)


def _sigmoid(x):
    return 1.0 / (1.0 + jnp.exp(-x))


def _layer_norm(z, g, b):
    mu = jnp.mean(z, axis=-1, keepdims=True)
    zc = z - mu
    var = jnp.mean(zc * zc, axis=-1, keepdims=True)
    return zc * lax.rsqrt(var + LN_EPS) * g + b


def _params(*sem):
    return pltpu.CompilerParams(dimension_semantics=sem, vmem_limit_bytes=VMEM_LIMIT)


def _mm_kernel(a_ref, b_ref, o_ref):
    o_ref[...] = _dot(a_ref[...].astype(BF16), b_ref[...])


def _matmul(a, b, tm):
    m, k = a.shape
    n = b.shape[1]
    return pl.pallas_call(
        _mm_kernel,
        grid=(m // tm,),
        in_specs=[pl.BlockSpec((tm, k), lambda i: (i, 0)),
                  pl.BlockSpec((k, n), lambda i: (0, 0))],
        out_specs=pl.BlockSpec((tm, n), lambda i: (i, 0)),
        out_shape=jax.ShapeDtypeStruct((m, n), F32),
        compiler_params=_params("arbitrary"),
        name="matmul",
    )(a, b)


def _rglru_kernel(x_ref, win_ref, cw_ref, cb_ref, wr_ref, br_ref, wi_ref, bi_ref, lam_ref,
                  tok_ref, mq_ref, tail_ref, h_ref, a_s, u_s, *, tok_w):
    s = pl.program_id(1)

    @pl.when(s == 0)
    def _():
        tail_ref[...] = jnp.zeros_like(tail_ref)
        h_ref[...] = jnp.zeros_like(h_ref)

    proj = _dot(x_ref[0].astype(BF16), win_ref[...])
    xb = proj[:, :tok_w]
    gb = proj[:, tok_w:2 * tok_w]
    mq_ref[0] = proj[:, 2 * tok_w:]
    ts = xb.shape[0]

    xe = jnp.concatenate([tail_ref[...], xb], axis=0)
    tail_ref[...] = xb[ts - SUBLANES:, :]
    xc = xb * cw_ref[CONV_W - 1:CONV_W, :] + cb_ref[...]
    for d in range(1, CONV_W):
        sh = pltpu.roll(xe, d, 0)[SUBLANES:, :]
        xc = xc + sh * cw_ref[CONV_W - 1 - d:CONV_W - d, :]

    xcb = xc.astype(BF16)
    r = _sigmoid(_dot(xcb, wr_ref[...]) + br_ref[...])
    ig = _sigmoid(_dot(xcb, wi_ref[...]) + bi_ref[...])
    nl = -lam_ref[...]
    softplus = jnp.maximum(nl, 0.0) + jnp.log1p(jnp.exp(-jnp.abs(nl)))
    log_a = (-LRU_C) * r * softplus
    a = jnp.exp(log_a)
    th = jnp.tanh(log_a)
    u = jnp.sqrt(-2.0 * th / (1.0 - th)) * (ig * xc)

    row = lax.broadcasted_iota(I32, a.shape, 0) & (SUBLANES - 1)
    for d in (1, 2, 4):
        m = row >= d
        u = jnp.where(m, a * pltpu.roll(u, d, 0) + u, u)
        a = jnp.where(m, a * pltpu.roll(a, d, 0), a)
    a_s[...] = a
    u_s[...] = u

    def group(g, h):
        sl = pl.ds(pl.multiple_of(g * SUBLANES, SUBLANES), SUBLANES)
        hg = u_s[sl, :] + a_s[sl, :] * h
        u_s[sl, :] = hg
        return jnp.broadcast_to(hg[SUBLANES - 1:, :], hg.shape)

    h_ref[...] = lax.fori_loop(0, ts // SUBLANES, group, h_ref[...])

    gelu = 0.5 * gb * (1.0 + jnp.tanh(math.sqrt(2.0 / math.pi) * (gb + 0.044715 * (gb * gb * gb))))
    tok_ref[0] = u_s[...] * gelu


def _rglru_front(x, w_in, conv_w, conv_b, wr, br, wi, bi, lam, ts):
    b, s, d = x.shape
    tok_w = conv_w.shape[-1]
    mem_w = w_in.shape[-1] - 2 * tok_w
    const = lambda shape: pl.BlockSpec(shape, lambda i, j: (0,) * len(shape))
    return pl.pallas_call(
        functools.partial(_rglru_kernel, tok_w=tok_w),
        grid=(b, s // ts),
        in_specs=[pl.BlockSpec((1, ts, d), lambda i, j: (i, j, 0)),
                  const(w_in.shape), const(conv_w.shape), const(conv_b.shape),
                  const(wr.shape), const(br.shape), const(wi.shape), const(bi.shape),
                  const(lam.shape)],
        out_specs=[pl.BlockSpec((1, ts, tok_w), lambda i, j: (i, j, 0)),
                   pl.BlockSpec((1, ts, mem_w), lambda i, j: (i, j, 0))],
        out_shape=[jax.ShapeDtypeStruct((b, s, tok_w), F32),
                   jax.ShapeDtypeStruct((b, s, mem_w), F32)],
        scratch_shapes=[pltpu.VMEM((SUBLANES, tok_w), F32), pltpu.VMEM((SUBLANES, tok_w), F32),
                        pltpu.VMEM((ts, tok_w), F32), pltpu.VMEM((ts, tok_w), F32)],
        compiler_params=_params("arbitrary", "arbitrary"),
        name="rglru_front",
    )(x, w_in, conv_w, conv_b, wr, br, wi, bi, lam)


def _dsa_layout(tok_w):
    q0 = 0
    c0 = q0 + tok_w
    iq0 = c0 + KV_LATENT
    mq0 = iq0 + IDX_HEADS * IDX_DIM
    ik0 = mq0 + MEM_WIDTH
    iw0 = ik0 + LANES
    total = iw0 + LANES
    return q0, c0, iq0, mq0, ik0, iw0, total


def _dsa_proj_kernel(x_ref, w_ref, wuk_ref, kvg_ref, ig_ref, ib_ref,
                     qlat_ref, c_ref, iq_ref, ik_ref, iw_ref, mq_ref, *, tok_w):
    q0, c0, iq0, mq0, ik0, iw0, _ = _dsa_layout(tok_w)
    proj = _dot(x_ref[0].astype(BF16), w_ref[...])
    for h in range(tok_w // HEAD_DIM):
        qh = proj[:, q0 + h * HEAD_DIM:q0 + (h + 1) * HEAD_DIM].astype(BF16)
        ql = _dot(qh, wuk_ref[h]) * HEAD_DIM ** -0.5
        qlat_ref[0, h] = ql.astype(BF16)
    c = proj[:, c0:c0 + KV_LATENT]
    c = c * lax.rsqrt(jnp.mean(c * c, axis=-1, keepdims=True) + RMS_EPS) * kvg_ref[...]
    c_ref[0] = c.astype(BF16)
    iq_ref[0] = proj[:, iq0:iq0 + IDX_HEADS * IDX_DIM].astype(BF16)
    ik = _layer_norm(proj[:, ik0:ik0 + IDX_DIM], ig_ref[...], ib_ref[...])
    ik_ref[0] = ik.astype(BF16)
    iw_ref[0] = proj[:, iw0:iw0 + IDX_HEADS] * (IDX_HEADS ** -0.5 * IDX_DIM ** -0.5)
    mq_ref[0] = proj[:, mq0:mq0 + MEM_WIDTH]


def _dsa_proj(x, w, wuk, kvg, ig, ib, ts):
    b, s, d = x.shape
    nh = wuk.shape[0]
    tok_w = nh * HEAD_DIM
    const = lambda shape: pl.BlockSpec(shape, lambda i, j: (0,) * len(shape))
    tile = lambda w_: pl.BlockSpec((1, ts, w_), lambda i, j: (i, j, 0))
    widths = [KV_LATENT, IDX_HEADS * IDX_DIM, IDX_DIM, IDX_HEADS, MEM_WIDTH]
    dtypes = [BF16, BF16, BF16, F32, F32]
    return pl.pallas_call(
        functools.partial(_dsa_proj_kernel, tok_w=tok_w),
        grid=(b, s // ts),
        in_specs=[tile(d), const(w.shape), const(wuk.shape), const(kvg.shape),
                  const(ig.shape), const(ib.shape)],
        out_specs=[pl.BlockSpec((1, nh, ts, KV_LATENT), lambda i, j: (i, 0, j, 0))]
        + [tile(w_) for w_ in widths],
        out_shape=[jax.ShapeDtypeStruct((b, nh, s, KV_LATENT), BF16)]
        + [jax.ShapeDtypeStruct((b, s, w_), t) for w_, t in zip(widths, dtypes)],
        compiler_params=_params("arbitrary", "arbitrary"),
        name="dsa_proj",
    )(x, w, wuk, kvg, ig, ib)


def _row_popcount(words):
    pc = lax.population_count(words)
    part = pc[:, :LANES]
    for i in range(1, words.shape[1] // LANES):
        part = part + pc[:, i * LANES:(i + 1) * LANES]
    return jnp.sum(part.astype(F32), axis=1, keepdims=True)


def _dsa_select_kernel(iq_ref, iw_ref, ikt_ref, mask_ref, keys_ref, iwb_ref, tie_ref, gt_ref,
                       *, k_sel, chunk):
    qi = pl.program_id(1)
    tq = iq_ref.shape[1]
    iq = iq_ref[0]
    iw = iw_ref[0]
    for h in range(IDX_HEADS):
        iwb_ref[h] = jnp.broadcast_to(iw[:, h:h + 1], (tq, chunk))
    iqh = [iq[:, h * IDX_DIM:(h + 1) * IDX_DIM] for h in range(IDX_HEADS)]

    def ukey(c):
        ikc = ikt_ref[0, c]
        sc = jnp.maximum(_dot(iqh[0], ikc), 0.0) * iwb_ref[0]
        for h in range(1, IDX_HEADS):
            sc = sc + jnp.maximum(_dot(iqh[h], ikc), 0.0) * iwb_ref[h]
        bits = lax.bitcast_convert_type(sc, I32)
        word = bits ^ ((bits >> 31) | INT_MIN)
        return word + (word == 0x7FFFFFFF).astype(I32)

    def score(c, _):
        keys_ref[c] = ukey(c)
        return 0

    lax.fori_loop(0, qi, score, 0)
    row = lax.broadcasted_iota(I32, (tq, chunk), 0)
    lane = lax.broadcasted_iota(I32, (tq, chunk), 1)
    keys_ref[qi] = jnp.where(lane <= row, ukey(qi), 0)

    def clear(c, _):
        keys_ref[c] = jnp.zeros((tq, chunk), I32)
        return 0

    lax.fori_loop(qi + 1, MASK_BITS, clear, 0)

    def transpose(g, _):
        rows = pl.ds(pl.multiple_of(g * SUBLANES, SUBLANES), SUBLANES)
        for half in range(chunk // LANES):
            cols = slice(half * LANES, (half + 1) * LANES)
            a = [keys_ref[c, rows, cols] for c in range(MASK_BITS)]
            j, m = MASK_BITS // 2, 0x0000FFFF
            while j:
                k = 0
                while k < MASK_BITS:
                    t = (lax.shift_right_logical(a[k], j) ^ a[k + j]) & m
                    a[k] = a[k] ^ (t << j)
                    a[k + j] = a[k + j] ^ t
                    k = (k + j + 1) & ~j
                j >>= 1
                m = (m ^ (m << j)) & 0xFFFFFFFF
                m = m - (1 << 32) if m >= (1 << 31) else m
            for c in range(MASK_BITS):
                keys_ref[c, rows, cols] = a[c]
        return 0

    lax.fori_loop(0, tq // SUBLANES, transpose, 0)

    rg = min(64, tq)
    tie_ref[...] = jnp.full((tq, chunk), -1, I32)
    gt_ref[...] = jnp.zeros((tq, chunk), I32)

    def plane_step(i, carry):
        need, any_one = carry
        plane = MASK_BITS - 1 - i
        need_out, any_out = [], []
        for r in range(tq // rg):
            rows = slice(r * rg, (r + 1) * rg)
            active = tie_ref[rows, :]
            ones = active & keys_ref[plane, rows, :]
            cnt = _row_popcount(ones)
            take = cnt >= need[rows]
            tie_ref[rows, :] = jnp.where(take, ones, active ^ ones)
            gt_ref[rows, :] = jnp.where(take, 0, ones) | gt_ref[rows, :]
            need_out.append(jnp.where(take, need[rows], need[rows] - cnt))
            any_out.append(jnp.where(take, 1.0, any_one[rows]))
        return jnp.concatenate(need_out, axis=0), jnp.concatenate(any_out, axis=0)

    need, any_one = lax.fori_loop(
        0, MASK_BITS, plane_step,
        (jnp.full((tq, 1), float(k_sel), F32), jnp.zeros((tq, 1), F32)))
    ties = jnp.where(any_one > 0.0, tie_ref[...], 0)
    greater = gt_ref[...]
    tie_ref[...] = ties

    @pl.when(jnp.max(_row_popcount(ties) - need) > 0.0)
    def _():
        shift = chunk.bit_length() - 1
        lane_i = lax.broadcasted_iota(I32, (tq, chunk), 1)

        def below(v, inclusive):
            vh = jnp.broadcast_to(v >> shift, (tq, chunk))
            vl = jnp.broadcast_to(v & (chunk - 1), (tq, chunk))
            top = jnp.int32(1) << vh
            edge = (lane_i <= vl) if inclusive else (lane_i < vl)
            return (top - 1) | jnp.where(edge, top, 0)

        nbits = (MASK_BITS * chunk).bit_length() - 1

        def idx_step(i, j):
            cand = j | (jnp.int32(1) << (nbits - 1 - i))
            cnt = _row_popcount(ties & below(cand, False))
            return jnp.where(cnt < need, cand, j)

        last = lax.fori_loop(0, nbits, idx_step, jnp.zeros((tq, 1), I32))
        tie_ref[...] = ties & below(last, True)

    mask_ref[0] = greater | tie_ref[...]


def _dsa_select(iq, iw, ikt, chunk, k_sel):
    b, s, _ = iq.shape
    tq = chunk
    return pl.pallas_call(
        functools.partial(_dsa_select_kernel, k_sel=k_sel, chunk=chunk),
        grid=(b, s // tq),
        in_specs=[pl.BlockSpec((1, tq, iq.shape[2]), lambda i, j: (i, j, 0)),
                  pl.BlockSpec((1, tq, iw.shape[2]), lambda i, j: (i, j, 0)),
                  pl.BlockSpec((1,) + ikt.shape[1:], lambda i, j: (i, 0, 0, 0))],
        out_specs=pl.BlockSpec((1, tq, chunk), lambda i, j: (i, j, 0)),
        out_shape=jax.ShapeDtypeStruct((b, s, chunk), I32),
        scratch_shapes=[pltpu.VMEM((MASK_BITS, tq, chunk), I32),
                        pltpu.VMEM((IDX_HEADS, tq, chunk), F32),
                        pltpu.VMEM((tq, chunk), I32), pltpu.VMEM((tq, chunk), I32)],
        compiler_params=_params("arbitrary", "arbitrary"),
        name="dsa_select",
    )(iq, iw, ikt)


def _dsa_attend_kernel(qi_ref, kb_ref, qlat_ref, c_ref, mask_ref, bias_ref, wuv_ref, out_ref,
                       m_ref, acc_ref, madd_ref, *, nh, rg):
    p_id = pl.program_id(1)
    qi = qi_ref[p_id]
    kb = kb_ref[p_id]
    tq, tk = madd_ref.shape
    ngroups = tq // rg

    @pl.when(kb == 0)
    def _():
        m_ref[...] = jnp.full(m_ref.shape, NEG_BIG, F32)
        acc_ref[...] = jnp.zeros_like(acc_ref)

    madd_ref[...] = jnp.where(((mask_ref[0] >> kb) & 1) == 1, 0.0, NEG_BIG)
    c = c_ref[0]
    c_one = jnp.concatenate([c, jnp.ones_like(c)], axis=1)

    def heads(with_bias):
        for i in range(nh * ngroups):
            h, g = divmod(i, ngroups)
            rows = pl.ds(i * rg, rg)
            qrows = pl.ds(g * rg, rg)
            lg = _dot_nt(qlat_ref[0, h, qrows, :], c) + madd_ref[qrows, :]
            if with_bias:
                lg = lg + bias_ref[0, h, qrows, :]
            m_old = m_ref[rows, :]
            m_new = jnp.maximum(m_old, jnp.max(lg, axis=1, keepdims=True))
            alpha = jnp.exp(m_old - m_new)
            p = jnp.exp(lg - jnp.concatenate([m_new] * (tk // LANES), axis=1))
            m_ref[rows, :] = m_new
            acc_ref[rows, :] = (jnp.concatenate([alpha, alpha], axis=1) * acc_ref[rows, :]
                                + _dot(p.astype(BF16), c_one))

    @pl.when(qi - kb < 2)
    def _():
        heads(True)

    @pl.when(qi - kb >= 2)
    def _():
        heads(False)

    @pl.when(kb == qi)
    def _():
        for j in range(nh // 2):
            r0, r1 = pl.ds(2 * j * tq, tq), pl.ds((2 * j + 1) * tq, tq)
            o = jnp.concatenate([acc_ref[r0, :KV_LATENT] / acc_ref[r0, KV_LATENT:],
                                 acc_ref[r1, :KV_LATENT] / acc_ref[r1, KV_LATENT:]], axis=1)
            out_ref[0, :, j * LANES:(j + 1) * LANES] = _dot(o.astype(BF16), wuv_ref[j])


def _dsa_attend(qlat, c, mask, bias, wuv2, tq):
    b, nh, s, _ = qlat.shape
    nq = s // tq
    rg = min(128, tq)
    qi_tab = np.concatenate([np.full(q + 1, q, np.int32) for q in range(nq)])
    kb_tab = np.concatenate([np.arange(q + 1, dtype=np.int32) for q in range(nq)])
    grid_spec = pltpu.PrefetchScalarGridSpec(
        num_scalar_prefetch=2,
        grid=(b, len(qi_tab)),
        in_specs=[pl.BlockSpec((1, nh, tq, KV_LATENT), lambda i, p, qt, kt: (i, 0, qt[p], 0)),
                  pl.BlockSpec((1, tq, c.shape[2]), lambda i, p, qt, kt: (i, kt[p], 0)),
                  pl.BlockSpec((1, tq, mask.shape[2]), lambda i, p, qt, kt: (i, qt[p], 0)),
                  pl.BlockSpec((1, nh, tq, tq),
                               lambda i, p, qt, kt: (jnp.minimum(qt[p] - kt[p], 1), 0, 0, 0)),
                  pl.BlockSpec(wuv2.shape, lambda i, p, qt, kt: (0, 0, 0))],
        out_specs=pl.BlockSpec((1, tq, nh * HEAD_DIM), lambda i, p, qt, kt: (i, qt[p], 0)),
        scratch_shapes=[pltpu.VMEM((nh * tq, LANES), F32),
                        pltpu.VMEM((nh * tq, 2 * KV_LATENT), F32),
                        pltpu.VMEM((tq, tq), F32)],
    )
    return pl.pallas_call(
        functools.partial(_dsa_attend_kernel, nh=nh, rg=rg),
        grid_spec=grid_spec,
        out_shape=jax.ShapeDtypeStruct((b, s, nh * HEAD_DIM), F32),
        compiler_params=_params("arbitrary", "arbitrary"),
        name="dsa_attend",
    )(jnp.asarray(qi_tab), jnp.asarray(kb_tab), qlat, c, mask, bias, wuv2)


def _t5_bucket(rel):
    n = jnp.maximum(rel, 0)
    max_exact = REL_BUCKETS // 2
    large = max_exact + (jnp.log(jnp.maximum(n, 1).astype(F32) / max_exact)
                         / math.log(REL_MAX_DIST / max_exact) * (REL_BUCKETS - max_exact)).astype(I32)
    large = jnp.minimum(large, REL_BUCKETS - 1)
    return jnp.where(n < max_exact, n, large)


def _bias_tiles(rel_bias, tq):
    assert tq >= REL_MAX_DIST
    i = jnp.arange(tq, dtype=I32)
    rel0 = i[:, None] - i[None, :]
    shifted = rel_bias - rel_bias[REL_BUCKETS - 1]
    buckets = jnp.stack([_t5_bucket(rel0 + d * tq) for d in range(2)])
    onehot = jax.nn.one_hot(buckets, REL_BUCKETS, dtype=F32)
    return jnp.einsum('dijb,bh->dhij', onehot, shifted, precision=lax.Precision.HIGHEST)


def _post_kernel(x_ref, tok_ref, mq_ref, kv_ref, wout_ref, g_ref, b_ref, x1_ref, *, tok_w):
    mq = mq_ref[0]
    kv = kv_ref[0]
    y = _dot(tok_ref[0].astype(BF16), wout_ref[:tok_w, :])
    for h in range(MEM_HEADS):
        q = mq[:, h * HEAD_DIM:(h + 1) * HEAD_DIM].astype(BF16)
        k = kv[:, h * HEAD_DIM:(h + 1) * HEAD_DIM]
        v = kv[:, MEM_WIDTH + h * HEAD_DIM:MEM_WIDTH + (h + 1) * HEAD_DIM]
        lg = _dot_nt(q, k) * HEAD_DIM ** -0.5
        p = jnp.exp(lg - jnp.max(lg, axis=1, keepdims=True)).astype(BF16)
        o = _dot(p, v) / _dot(p, jnp.ones((p.shape[1], LANES), BF16))[:, :HEAD_DIM]
        y = y + _dot(o.astype(BF16), wout_ref[tok_w + h * HEAD_DIM:tok_w + (h + 1) * HEAD_DIM, :])
    x1_ref[0] = _layer_norm(DN_ALPHA * x_ref[0] + y, g_ref[...], b_ref[...])


def _post_mixer(x, tok, mq, kv, w_out, g, b, ts):
    bsz, s, d = x.shape
    tok_w = tok.shape[2]
    const = lambda shape: pl.BlockSpec(shape, lambda i, j: (0,) * len(shape))
    tile = lambda w_: pl.BlockSpec((1, ts, w_), lambda i, j: (i, j, 0))
    return pl.pallas_call(
        functools.partial(_post_kernel, tok_w=tok_w),
        grid=(bsz, s // ts),
        in_specs=[tile(d), tile(tok_w), tile(mq.shape[2]),
                  pl.BlockSpec((1,) + kv.shape[1:], lambda i, j: (i, 0, 0)),
                  const(w_out.shape), const(g.shape), const(b.shape)],
        out_specs=tile(d),
        out_shape=jax.ShapeDtypeStruct((bsz, s, d), F32),
        compiler_params=_params("arbitrary", "arbitrary"),
        name="post_mixer",
    )(x, tok, mq, kv, w_out, g, b)


def _router_kernel(x_ref, wh_ref, wl_ref, rb_ref, e_ref, gate_ref, rank_ref, cnt_ref, carry_ref):
    @pl.when(pl.program_id(0) == 0)
    def _():
        carry_ref[...] = jnp.zeros_like(carry_ref)

    x = x_ref[...]
    tr = x.shape[0]
    xh = x.astype(BF16)
    xl = (x - xh.astype(F32)).astype(BF16)
    lg = _dot(xh, wh_ref[...]) + (_dot(xl, wh_ref[...]) + _dot(xh, wl_ref[...])) + rb_ref[...]
    lane = lax.broadcasted_iota(I32, lg.shape, 1).astype(F32)
    lg = jnp.where(lane < N_EXPERTS, lg, -jnp.inf)

    idx, val = [], []
    onehot = jnp.zeros(lg.shape, F32)
    for _ in range(TOP_K):
        m = jnp.max(lg, axis=1, keepdims=True)
        i = jnp.min(jnp.where(lg == m, lane, float(LANES)), axis=1, keepdims=True)
        hit = lane == i
        idx.append(i)
        val.append(m)
        onehot = jnp.where(hit, 1.0, onehot)
        lg = jnp.where(hit, -jnp.inf, lg)

    ex = [jnp.exp(v - val[0]) for v in val]
    den = ex[0] + ex[1] + ex[2] + ex[3]

    r_i = lax.broadcasted_iota(I32, (tr, tr), 0)
    c_i = lax.broadcasted_iota(I32, (tr, tr), 1)
    tri = jnp.where(c_i < r_i, 1.0, 0.0).astype(BF16)
    before = _dot(tri, onehot.astype(BF16)) + carry_ref[...]
    carry_ref[...] = carry_ref[...] + jnp.sum(onehot, axis=0, keepdims=True)
    cnt_ref[...] = carry_ref[...]

    lane_k = lax.broadcasted_iota(I32, (tr, TOP_K), 1)
    e_out = jnp.zeros((tr, TOP_K), I32)
    g_out = jnp.zeros((tr, TOP_K), F32)
    r_out = jnp.zeros((tr, TOP_K), I32)
    for k in range(TOP_K):
        rk = jnp.sum(jnp.where(lane == idx[k], before, 0.0), axis=1, keepdims=True).astype(I32)
        e_out = jnp.where(lane_k == k, idx[k].astype(I32), e_out)
        g_out = jnp.where(lane_k == k, ex[k] / den, g_out)
        r_out = jnp.where(lane_k == k, rk, r_out)
    e_ref[...] = e_out
    gate_ref[...] = g_out
    rank_ref[...] = r_out


def _router(x, wh, wl, rb, tr):
    t, d = x.shape
    const = lambda shape: pl.BlockSpec(shape, lambda i: (0,) * len(shape))
    small = pl.BlockSpec((tr, TOP_K), lambda i: (i, 0))
    return pl.pallas_call(
        _router_kernel,
        grid=(t // tr,),
        in_specs=[pl.BlockSpec((tr, d), lambda i: (i, 0)), const(wh.shape), const(wl.shape),
                  const(rb.shape)],
        out_specs=[small, small, small, const((1, LANES))],
        out_shape=[jax.ShapeDtypeStruct((t, TOP_K), I32), jax.ShapeDtypeStruct((t, TOP_K), F32),
                   jax.ShapeDtypeStruct((t, TOP_K), I32), jax.ShapeDtypeStruct((1, LANES), F32)],
        scratch_shapes=[pltpu.VMEM((1, LANES), F32)],
        compiler_params=_params("arbitrary"),
        name="router",
    )(x, wh, wl, rb)


def _dest_kernel(e_ref, rank_ref, start_ref, dest_ref):
    e = e_ref[...]
    lane = lax.broadcasted_iota(I32, (e.shape[0], LANES), 1)
    lane_k = lax.broadcasted_iota(I32, e.shape, 1)
    out = rank_ref[...]
    for k in range(TOP_K):
        st = jnp.sum(jnp.where(lane == e[:, k:k + 1], start_ref[...], 0.0), axis=1, keepdims=True)
        out = out + jnp.where(lane_k == k, st.astype(I32), 0)
    dest_ref[...] = out


def _dest(e, rank, start, tr):
    t = e.shape[0]
    small = pl.BlockSpec((tr, TOP_K), lambda i: (i, 0))
    return pl.pallas_call(
        _dest_kernel,
        grid=(t // tr,),
        in_specs=[small, small, pl.BlockSpec((1, LANES), lambda i: (0, 0))],
        out_specs=small,
        out_shape=jax.ShapeDtypeStruct((t, TOP_K), I32),
        compiler_params=_params("arbitrary"),
        name="dest",
    )(e, rank, start)


def _dispatch_kernel(dest_ref, x_ref, xs_in_ref, xs_ref, sem):
    del xs_in_ref
    tt = x_ref.shape[0]

    def row_copy(i, k):
        d = dest_ref[i * TOP_K + k]
        return pltpu.make_async_copy(x_ref.at[pl.ds(i, 1)], xs_ref.at[pl.ds(d, 1)], sem)

    def issue(i, _):
        for k in range(TOP_K):
            row_copy(i, k).start()
        return 0

    def drain(i, _):
        for k in range(TOP_K):
            row_copy(i, k).wait()
        return 0

    lax.fori_loop(0, tt, issue, 0, unroll=8)
    lax.fori_loop(0, tt, drain, 0, unroll=8)


def _dispatch(dest_flat, x, rows, tt):
    t, d = x.shape
    xs0 = jnp.zeros((rows, d), x.dtype)
    return pl.pallas_call(
        _dispatch_kernel,
        grid=(t // tt,),
        in_specs=[pl.BlockSpec((tt * TOP_K,), lambda i: (i,), memory_space=pltpu.SMEM),
                  pl.BlockSpec((tt, d), lambda i: (i, 0)),
                  pl.BlockSpec(memory_space=pl.ANY)],
        out_specs=pl.BlockSpec(memory_space=pl.ANY),
        out_shape=jax.ShapeDtypeStruct((rows, d), x.dtype),
        scratch_shapes=[pltpu.SemaphoreType.DMA(())],
        input_output_aliases={2: 0},
        compiler_params=_params("arbitrary"),
        name="dispatch",
    )(dest_flat, x, xs0)


def _ffn_kernel(be_ref, xs_ref, w1_ref, b1_ref, w2_ref, b2_ref, ys_ref):
    del be_ref
    de = w2_ref.shape[1]
    h = _dot(xs_ref[...].astype(BF16), w1_ref[0]) + b1_ref[0]
    gt = jnp.minimum(h[:, :de], SWIGLU_LIMIT)
    up = jnp.clip(h[:, de:], -SWIGLU_LIMIT, SWIGLU_LIMIT)
    act = (up + 1.0) * (gt * _sigmoid(SWIGLU_ALPHA * gt))
    ys_ref[...] = _dot(act.astype(BF16), w2_ref[0]) + b2_ref[0]


def _expert_ffn(blk_e, xs, w1, b1, w2, b2, tm):
    rows, d = xs.shape
    grid_spec = pltpu.PrefetchScalarGridSpec(
        num_scalar_prefetch=1,
        grid=(rows // tm,),
        in_specs=[pl.BlockSpec((tm, d), lambda i, be: (i, 0)),
                  pl.BlockSpec((1,) + w1.shape[1:], lambda i, be: (be[i], 0, 0)),
                  pl.BlockSpec((1,) + b1.shape[1:], lambda i, be: (be[i], 0, 0)),
                  pl.BlockSpec((1,) + w2.shape[1:], lambda i, be: (be[i], 0, 0)),
                  pl.BlockSpec((1,) + b2.shape[1:], lambda i, be: (be[i], 0, 0))],
        out_specs=pl.BlockSpec((tm, d), lambda i, be: (i, 0)),
    )
    return pl.pallas_call(
        _ffn_kernel,
        grid_spec=grid_spec,
        out_shape=jax.ShapeDtypeStruct((rows, d), F32),
        compiler_params=_params("arbitrary"),
        name="expert_ffn",
    )(blk_e, xs, w1, b1, w2, b2)


def _combine_kernel(dest_ref, x_ref, gate_ref, g_ref, b_ref, ys_ref, out_ref, buf, sem):
    tt = x_ref.shape[0]

    def row_copy(i, k):
        d = dest_ref[i * TOP_K + k]
        return pltpu.make_async_copy(ys_ref.at[pl.ds(d, 1)], buf.at[k, pl.ds(i, 1)], sem)

    def issue(i, _):
        for k in range(TOP_K):
            row_copy(i, k).start()
        return 0

    def drain(i, _):
        for k in range(TOP_K):
            row_copy(i, k).wait()
        return 0

    lax.fori_loop(0, tt, issue, 0, unroll=8)
    lax.fori_loop(0, tt, drain, 0, unroll=8)

    gate = gate_ref[...]
    ffn = gate[:, 0:1] * buf[0]
    for k in range(1, TOP_K):
        ffn = ffn + gate[:, k:k + 1] * buf[k]
    out_ref[...] = _layer_norm(DN_ALPHA * x_ref[...] + ffn, g_ref[...], b_ref[...])


def _combine(dest_flat, x, gate, g, b, ys, tt):
    t, d = x.shape
    const = lambda shape: pl.BlockSpec(shape, lambda i: (0,) * len(shape))
    return pl.pallas_call(
        _combine_kernel,
        grid=(t // tt,),
        in_specs=[pl.BlockSpec((tt * TOP_K,), lambda i: (i,), memory_space=pltpu.SMEM),
                  pl.BlockSpec((tt, d), lambda i: (i, 0)),
                  pl.BlockSpec((tt, TOP_K), lambda i: (i, 0)),
                  const(g.shape), const(b.shape),
                  pl.BlockSpec(memory_space=pl.ANY)],
        out_specs=pl.BlockSpec((tt, d), lambda i: (i, 0)),
        out_shape=jax.ShapeDtypeStruct((t, d), F32),
        scratch_shapes=[pltpu.VMEM((TOP_K, tt, d), F32), pltpu.SemaphoreType.DMA(())],
        compiler_params=_params("arbitrary"),
        name="combine",
    )(dest_flat, x, gate, g, b, ys)


def _moe_layer(x, router_w, router_b, w1, b1, w2, b2, g, b, tiles):
    t, d = x.shape
    tm, tt, tr = tiles["tm"], tiles["tt"], tiles["tr"]
    wpad = jnp.pad(router_w, ((0, 0), (0, LANES - N_EXPERTS)))
    wh = wpad.astype(BF16)
    wl = (wpad - wh.astype(F32)).astype(BF16)
    rb = jnp.pad(router_b, (0, LANES - N_EXPERTS))[None, :]
    e, gate, rank, cnt = _router(x, wh, wl, rb, tr)

    counts = cnt[0, :N_EXPERTS].astype(I32)
    padded = (counts + tm - 1) // tm * tm
    pend = jnp.cumsum(padded)
    start = jnp.pad(pend - padded, (0, LANES - N_EXPERTS))[None, :].astype(F32)
    ntile = t * TOP_K // tm + N_EXPERTS
    tile_row = jnp.arange(ntile, dtype=I32) * tm
    blk_e = jnp.minimum(jnp.sum(pend[None, :] <= tile_row[:, None], axis=1), N_EXPERTS - 1).astype(I32)

    dest = _dest(e, rank, start, tr).reshape(-1)
    xs = _dispatch(dest, x, ntile * tm, tt)
    ys = _expert_ffn(blk_e, xs, w1.astype(BF16), b1[:, None, :], w2.astype(BF16), b2[:, None, :], tm)
    return _combine(dest, x, gate, g[None, :], b[None, :], ys, tt)


def _block_diag(w):
    n, c, _ = w.shape
    eye = jnp.eye(n, dtype=w.dtype)
    return (eye[:, None, :, None] * w[:, :, None, :]).reshape(n * c, n * c)


def kernel(x, mem, rel_bias, a_w_in, a_conv_w, a_conv_b, a_wr, a_br, a_wi, a_bi, a_lambda, b_w_in, b_kv_norm_g, b_w_uk, b_w_uv, b_idx_norm_g, b_idx_norm_b, w_mem_kv, w_out, ln1_g, ln1_b, router_w, router_b, exp_w1, exp_b1, exp_w2, exp_b2, ln2_g, ln2_b):
    bsz, seq, d = x.shape
    tiles = _tiles(seq)
    ts = tiles["ts"]
    row = lambda v: v[None, :]

    def finish_layer(layer, x, tok, mq):
        kv = _matmul(mem.reshape(-1, d), w_mem_kv[layer].astype(BF16), mem.shape[1])
        kv = kv.astype(BF16).reshape(bsz, mem.shape[1], -1)
        x1 = _post_mixer(x, tok, mq, kv, w_out[layer].astype(BF16), row(ln1_g[layer]),
                         row(ln1_b[layer]), ts)
        x2 = _moe_layer(x1.reshape(-1, d), router_w[layer], router_b[layer], exp_w1[layer],
                        exp_b1[layer], exp_w2[layer], exp_b2[layer], ln2_g[layer], ln2_b[layer],
                        tiles)
        return x2.reshape(bsz, seq, d)

    tok, mq = _rglru_front(x, a_w_in[0].astype(BF16), a_conv_w[0], row(a_conv_b[0]),
                           _block_diag(a_wr[0]).astype(BF16), row(a_br[0]),
                           _block_diag(a_wi[0]).astype(BF16), row(a_bi[0]), row(a_lambda[0]), ts)
    x = finish_layer(0, x, tok, mq)

    nh = b_w_uk.shape[2]
    tok_w = nh * HEAD_DIM
    q0, c0, iq0, mq0, ik0, iw0, total = _dsa_layout(tok_w)
    w = b_w_in[0]
    o_c, o_iq = tok_w, tok_w + KV_LATENT
    o_ik = o_iq + IDX_HEADS * IDX_DIM
    o_iw = o_ik + IDX_DIM
    o_mq = o_iw + IDX_HEADS
    w_re = jnp.zeros((d, total), F32)
    w_re = w_re.at[:, q0:q0 + tok_w].set(w[:, :tok_w])
    w_re = w_re.at[:, c0:c0 + KV_LATENT].set(w[:, o_c:o_iq])
    w_re = w_re.at[:, iq0:iq0 + IDX_HEADS * IDX_DIM].set(w[:, o_iq:o_ik])
    w_re = w_re.at[:, ik0:ik0 + IDX_DIM].set(w[:, o_ik:o_iw])
    w_re = w_re.at[:, iw0:iw0 + IDX_HEADS].set(w[:, o_iw:o_mq])
    w_re = w_re.at[:, mq0:mq0 + MEM_WIDTH].set(w[:, o_mq:])
    wuk = jnp.transpose(b_w_uk[0], (1, 2, 0)).astype(BF16)
    wuv = jnp.transpose(b_w_uv[0], (1, 0, 2))
    wuv2 = jnp.zeros((nh // 2, 2 * KV_LATENT, 2 * HEAD_DIM), F32)
    wuv2 = wuv2.at[:, :KV_LATENT, :HEAD_DIM].set(wuv[0::2])
    wuv2 = wuv2.at[:, KV_LATENT:, HEAD_DIM:].set(wuv[1::2]).astype(BF16)
    qlat, c, iq, ik, iw, mq = _dsa_proj(x, w_re.astype(BF16), wuk, row(b_kv_norm_g[0]),
                                        row(b_idx_norm_g[0]), row(b_idx_norm_b[0]), ts)
    k_sel = min(TOPK_MAX, seq // 4)
    chunk = tiles["chunk"]
    ikt = jnp.swapaxes(ik.reshape(bsz, MASK_BITS, chunk, IDX_DIM), 2, 3)
    mask = _dsa_select(iq, iw, ikt, chunk, k_sel)
    tok = _dsa_attend(qlat, c, mask, _bias_tiles(rel_bias, tiles["tq_att"]), wuv2, tiles["tq_att"])
    x = finish_layer(1, x, tok, mq)
    return x
```

```python
import functools
import math

import jax
import jax.numpy as jnp
import numpy as np
from jax import lax
from jax.experimental import pallas as pl
from jax.experimental.pallas import tpu as pltpu

F32 = jnp.float32
BF16 = jnp.bfloat16
I32 = jnp.int32

HEAD_DIM = 64
MEM_HEADS = 4
MEM_WIDTH = MEM_HEADS * HEAD_DIM
LRU_C = 8.0
CONV_W = 4
KV_LATENT = 128
IDX_HEADS = 4
IDX_DIM = 64
TOPK_MAX = 256
REL_BUCKETS = 32
REL_MAX_DIST = 128
N_EXPERTS = 32
TOP_K = 4
SWIGLU_LIMIT = 7.0
SWIGLU_ALPHA = 1.702
DEPTH = 2
DN_ALPHA = (2 * DEPTH) ** 0.25
LN_EPS = 1e-5
RMS_EPS = 1e-6

SUBLANES = 8
LANES = 128
MASK_BITS = 32
INT_MIN = -2 ** 31
NEG_BIG = -1e30
VMEM_LIMIT = 56 * 1024 * 1024


def _tiles(seq):
    chunk = seq // MASK_BITS
    return dict(
        ts=min(512, seq),
        chunk=chunk,
        tq_att=2 * chunk,
        tt=256,
        tm=512,
        tr=512,
    )


def _dot(a, b):
    return jnp.dot(a, b, preferred_element_type=F32)


def _dot_nt(a, b):
    return lax.dot_general(a, b, (((1,), (1,)), ((), ())), preferred_element_type=F32)


def _sigmoid(x):
    return 1.0 / (1.0 + jnp.exp(-x))


def _layer_norm(z, g, b):
    mu = jnp.mean(z, axis=-1, keepdims=True)
    zc = z - mu
    var = jnp.mean(zc * zc, axis=-1, keepdims=True)
    return zc * lax.rsqrt(var + LN_EPS) * g + b


def _params(*sem):
    return pltpu.CompilerParams(dimension_semantics=sem, vmem_limit_bytes=VMEM_LIMIT)


def _mm_kernel(a_ref, b_ref, o_ref):
    o_ref[...] = _dot(a_ref[...].astype(BF16), b_ref[...])


def _matmul(a, b, tm):
    m, k = a.shape
    n = b.shape[1]
    return pl.pallas_call(
        _mm_kernel,
        grid=(m // tm,),
        in_specs=[pl.BlockSpec((tm, k), lambda i: (i, 0)),
                  pl.BlockSpec((k, n), lambda i: (0, 0))],
        out_specs=pl.BlockSpec((tm, n), lambda i: (i, 0)),
        out_shape=jax.ShapeDtypeStruct((m, n), F32),
        compiler_params=_params("arbitrary"),
        name="matmul",
    )(a, b)


def _rglru_kernel(x_ref, win_ref, cw_ref, cb_ref, wr_ref, br_ref, wi_ref, bi_ref, lam_ref,
                  tok_ref, mq_ref, tail_ref, h_ref, a_s, u_s, *, tok_w):
    s = pl.program_id(1)

    @pl.when(s == 0)
    def _():
        tail_ref[...] = jnp.zeros_like(tail_ref)
        h_ref[...] = jnp.zeros_like(h_ref)

    proj = _dot(x_ref[0].astype(BF16), win_ref[...])
    xb = proj[:, :tok_w]
    gb = proj[:, tok_w:2 * tok_w]
    mq_ref[0] = proj[:, 2 * tok_w:]
    ts = xb.shape[0]

    xe = jnp.concatenate([tail_ref[...], xb], axis=0)
    tail_ref[...] = xb[ts - SUBLANES:, :]
    xc = xb * cw_ref[CONV_W - 1:CONV_W, :] + cb_ref[...]
    for d in range(1, CONV_W):
        sh = pltpu.roll(xe, d, 0)[SUBLANES:, :]
        xc = xc + sh * cw_ref[CONV_W - 1 - d:CONV_W - d, :]

    xcb = xc.astype(BF16)
    r = _sigmoid(_dot(xcb, wr_ref[...]) + br_ref[...])
    ig = _sigmoid(_dot(xcb, wi_ref[...]) + bi_ref[...])
    nl = -lam_ref[...]
    softplus = jnp.maximum(nl, 0.0) + jnp.log1p(jnp.exp(-jnp.abs(nl)))
    log_a = (-LRU_C) * r * softplus
    a = jnp.exp(log_a)
    th = jnp.tanh(log_a)
    u = jnp.sqrt(-2.0 * th / (1.0 - th)) * (ig * xc)

    row = lax.broadcasted_iota(I32, a.shape, 0) & (SUBLANES - 1)
    for d in (1, 2, 4):
        m = row >= d
        u = jnp.where(m, a * pltpu.roll(u, d, 0) + u, u)
        a = jnp.where(m, a * pltpu.roll(a, d, 0), a)
    a_s[...] = a
    u_s[...] = u

    def group(g, h):
        sl = pl.ds(pl.multiple_of(g * SUBLANES, SUBLANES), SUBLANES)
        hg = u_s[sl, :] + a_s[sl, :] * h
        u_s[sl, :] = hg
        return jnp.broadcast_to(hg[SUBLANES - 1:, :], hg.shape)

    h_ref[...] = lax.fori_loop(0, ts // SUBLANES, group, h_ref[...])

    gelu = 0.5 * gb * (1.0 + jnp.tanh(math.sqrt(2.0 / math.pi) * (gb + 0.044715 * (gb * gb * gb))))
    tok_ref[0] = u_s[...] * gelu


def _rglru_front(x, w_in, conv_w, conv_b, wr, br, wi, bi, lam, ts):
    b, s, d = x.shape
    tok_w = conv_w.shape[-1]
    mem_w = w_in.shape[-1] - 2 * tok_w
    const = lambda shape: pl.BlockSpec(shape, lambda i, j: (0,) * len(shape))
    return pl.pallas_call(
        functools.partial(_rglru_kernel, tok_w=tok_w),
        grid=(b, s // ts),
        in_specs=[pl.BlockSpec((1, ts, d), lambda i, j: (i, j, 0)),
                  const(w_in.shape), const(conv_w.shape), const(conv_b.shape),
                  const(wr.shape), const(br.shape), const(wi.shape), const(bi.shape),
                  const(lam.shape)],
        out_specs=[pl.BlockSpec((1, ts, tok_w), lambda i, j: (i, j, 0)),
                   pl.BlockSpec((1, ts, mem_w), lambda i, j: (i, j, 0))],
        out_shape=[jax.ShapeDtypeStruct((b, s, tok_w), F32),
                   jax.ShapeDtypeStruct((b, s, mem_w), F32)],
        scratch_shapes=[pltpu.VMEM((SUBLANES, tok_w), F32), pltpu.VMEM((SUBLANES, tok_w), F32),
                        pltpu.VMEM((ts, tok_w), F32), pltpu.VMEM((ts, tok_w), F32)],
        compiler_params=_params("arbitrary", "arbitrary"),
        name="rglru_front",
    )(x, w_in, conv_w, conv_b, wr, br, wi, bi, lam)


def _dsa_layout(tok_w):
    q0 = 0
    c0 = q0 + tok_w
    iq0 = c0 + KV_LATENT
    mq0 = iq0 + IDX_HEADS * IDX_DIM
    ik0 = mq0 + MEM_WIDTH
    iw0 = ik0 + LANES
    total = iw0 + LANES
    return q0, c0, iq0, mq0, ik0, iw0, total


def _dsa_proj_kernel(x_ref, w_ref, wuk_ref, kvg_ref, ig_ref, ib_ref,
                     qlat_ref, c_ref, iq_ref, ik_ref, iw_ref, mq_ref, *, tok_w):
    q0, c0, iq0, mq0, ik0, iw0, _ = _dsa_layout(tok_w)
    proj = _dot(x_ref[0].astype(BF16), w_ref[...])
    for h in range(tok_w // HEAD_DIM):
        qh = proj[:, q0 + h * HEAD_DIM:q0 + (h + 1) * HEAD_DIM].astype(BF16)
        ql = _dot(qh, wuk_ref[h]) * HEAD_DIM ** -0.5
        qlat_ref[0, h] = ql.astype(BF16)
    c = proj[:, c0:c0 + KV_LATENT]
    c = c * lax.rsqrt(jnp.mean(c * c, axis=-1, keepdims=True) + RMS_EPS) * kvg_ref[...]
    c_ref[0] = c.astype(BF16)
    iq_ref[0] = proj[:, iq0:iq0 + IDX_HEADS * IDX_DIM].astype(BF16)
    ik = _layer_norm(proj[:, ik0:ik0 + IDX_DIM], ig_ref[...], ib_ref[...])
    ik_ref[0] = ik.astype(BF16)
    iw_ref[0] = proj[:, iw0:iw0 + IDX_HEADS] * (IDX_HEADS ** -0.5 * IDX_DIM ** -0.5)
    mq_ref[0] = proj[:, mq0:mq0 + MEM_WIDTH]


def _dsa_proj(x, w, wuk, kvg, ig, ib, ts):
    b, s, d = x.shape
    nh = wuk.shape[0]
    tok_w = nh * HEAD_DIM
    const = lambda shape: pl.BlockSpec(shape, lambda i, j: (0,) * len(shape))
    tile = lambda w_: pl.BlockSpec((1, ts, w_), lambda i, j: (i, j, 0))
    widths = [KV_LATENT, IDX_HEADS * IDX_DIM, IDX_DIM, IDX_HEADS, MEM_WIDTH]
    dtypes = [BF16, BF16, BF16, F32, F32]
    return pl.pallas_call(
        functools.partial(_dsa_proj_kernel, tok_w=tok_w),
        grid=(b, s // ts),
        in_specs=[tile(d), const(w.shape), const(wuk.shape), const(kvg.shape),
                  const(ig.shape), const(ib.shape)],
        out_specs=[pl.BlockSpec((1, nh, ts, KV_LATENT), lambda i, j: (i, 0, j, 0))]
        + [tile(w_) for w_ in widths],
        out_shape=[jax.ShapeDtypeStruct((b, nh, s, KV_LATENT), BF16)]
        + [jax.ShapeDtypeStruct((b, s, w_), t) for w_, t in zip(widths, dtypes)],
        compiler_params=_params("arbitrary", "arbitrary"),
        name="dsa_proj",
    )(x, w, wuk, kvg, ig, ib)


def _row_popcount(words):
    pc = lax.population_count(words)
    part = pc[:, :LANES]
    for i in range(1, words.shape[1] // LANES):
        part = part + pc[:, i * LANES:(i + 1) * LANES]
    return jnp.sum(part.astype(F32), axis=1, keepdims=True)


def _dsa_select_kernel(iq_ref, iw_ref, ikt_ref, mask_ref, keys_ref, iwb_ref, tie_ref,
                       *, k_sel, chunk):
    qi = pl.program_id(1)
    tq = iq_ref.shape[1]
    iq = iq_ref[0]
    iw = iw_ref[0]
    for h in range(IDX_HEADS):
        iwb_ref[h] = jnp.broadcast_to(iw[:, h:h + 1], (tq, chunk))
    iqh = [iq[:, h * IDX_DIM:(h + 1) * IDX_DIM] for h in range(IDX_HEADS)]

    def ukey(c):
        ikc = ikt_ref[0, c]
        sc = jnp.maximum(_dot(iqh[0], ikc), 0.0) * iwb_ref[0]
        for h in range(1, IDX_HEADS):
            sc = sc + jnp.maximum(_dot(iqh[h], ikc), 0.0) * iwb_ref[h]
        bits = lax.bitcast_convert_type(sc, I32)
        word = bits ^ ((bits >> 31) | INT_MIN)
        return word + (word == 0x7FFFFFFF).astype(I32)

    def score(c, _):
        keys_ref[c] = ukey(c)
        return 0

    lax.fori_loop(0, qi, score, 0)
    row = lax.broadcasted_iota(I32, (tq, chunk), 0)
    lane = lax.broadcasted_iota(I32, (tq, chunk), 1)
    keys_ref[qi] = jnp.where(lane <= row, ukey(qi), 0)

    def clear(c, _):
        keys_ref[c] = jnp.zeros((tq, chunk), I32)
        return 0

    lax.fori_loop(qi + 1, MASK_BITS, clear, 0)

    def transpose(g, _):
        rows = pl.ds(pl.multiple_of(g * SUBLANES, SUBLANES), SUBLANES)
        for half in range(chunk // LANES):
            cols = slice(half * LANES, (half + 1) * LANES)
            a = [keys_ref[c, rows, cols] for c in range(MASK_BITS)]
            j, m = MASK_BITS // 2, 0x0000FFFF
            while j:
                k = 0
                while k < MASK_BITS:
                    t = (lax.shift_right_logical(a[k], j) ^ a[k + j]) & m
                    a[k] = a[k] ^ (t << j)
                    a[k + j] = a[k + j] ^ t
                    k = (k + j + 1) & ~j
                j >>= 1
                m = (m ^ (m << j)) & 0xFFFFFFFF
                m = m - (1 << 32) if m >= (1 << 31) else m
            for c in range(MASK_BITS):
                keys_ref[c, rows, cols] = a[c]
        return 0

    lax.fori_loop(0, tq // SUBLANES, transpose, 0)

    def plane_step(i, carry):
        active, greater, need, any_one = carry
        ones = active & keys_ref[MASK_BITS - 1 - i]
        cnt = _row_popcount(ones)
        take = cnt >= need
        active = jnp.where(take, ones, active ^ ones)
        greater = jnp.where(take, greater, greater | ones)
        need = jnp.where(take, need, need - cnt)
        return active, greater, need, jnp.where(take, 1.0, any_one)

    full = jnp.full((tq, chunk), -1, I32)
    ties, greater, need, any_one = lax.fori_loop(
        0, MASK_BITS, plane_step,
        (full, jnp.zeros((tq, chunk), I32), jnp.full((tq, 1), float(k_sel), F32),
         jnp.zeros((tq, 1), F32)))
    ties = jnp.where(any_one > 0.0, ties, 0)
    tie_ref[...] = ties

    @pl.when(jnp.max(_row_popcount(ties) - need) > 0.0)
    def _():
        shift = chunk.bit_length() - 1
        lane_i = lax.broadcasted_iota(I32, (tq, chunk), 1)

        def below(v, inclusive):
            vh = jnp.broadcast_to(v >> shift, (tq, chunk))
            vl = jnp.broadcast_to(v & (chunk - 1), (tq, chunk))
            top = jnp.int32(1) << vh
            edge = (lane_i <= vl) if inclusive else (lane_i < vl)
            return (top - 1) | jnp.where(edge, top, 0)

        nbits = (MASK_BITS * chunk).bit_length() - 1

        def idx_step(i, j):
            cand = j | (jnp.int32(1) << (nbits - 1 - i))
            cnt = _row_popcount(ties & below(cand, False))
            return jnp.where(cnt < need, cand, j)

        last = lax.fori_loop(0, nbits, idx_step, jnp.zeros((tq, 1), I32))
        tie_ref[...] = ties & below(last, True)

    mask_ref[0] = greater | tie_ref[...]


def _dsa_select(iq, iw, ikt, chunk, k_sel):
    b, s, _ = iq.shape
    tq = chunk
    return pl.pallas_call(
        functools.partial(_dsa_select_kernel, k_sel=k_sel, chunk=chunk),
        grid=(b, s // tq),
        in_specs=[pl.BlockSpec((1, tq, iq.shape[2]), lambda i, j: (i, j, 0)),
                  pl.BlockSpec((1, tq, iw.shape[2]), lambda i, j: (i, j, 0)),
                  pl.BlockSpec((1,) + ikt.shape[1:], lambda i, j: (i, 0, 0, 0))],
        out_specs=pl.BlockSpec((1, tq, chunk), lambda i, j: (i, j, 0)),
        out_shape=jax.ShapeDtypeStruct((b, s, chunk), I32),
        scratch_shapes=[pltpu.VMEM((MASK_BITS, tq, chunk), I32),
                        pltpu.VMEM((IDX_HEADS, tq, chunk), F32),
                        pltpu.VMEM((tq, chunk), I32)],
        compiler_params=_params("arbitrary", "arbitrary"),
        name="dsa_select",
    )(iq, iw, ikt)


def _dsa_attend_kernel(qi_ref, kb_ref, qlat_ref, c_ref, mask_ref, bias_ref, wuv_ref, out_ref,
                       m_ref, acc_ref, madd_ref, *, nh, rg):
    p_id = pl.program_id(1)
    qi = qi_ref[p_id]
    kb = kb_ref[p_id]
    tq, tk = madd_ref.shape
    ngroups = tq // rg
    qmul = tq // tk
    first_near = qmul * qi - 1

    @pl.when(kb == 0)
    def _():
        m_ref[...] = jnp.full(m_ref.shape, NEG_BIG, F32)
        acc_ref[...] = jnp.zeros_like(acc_ref)

    madd_ref[...] = jnp.where(((mask_ref[0] >> kb) & 1) == 1, 0.0, NEG_BIG)
    c = c_ref[0]
    c_one = jnp.concatenate([c, jnp.ones_like(c)], axis=1)

    def heads(with_bias):
        for i in range(nh * ngroups):
            h, g = divmod(i, ngroups)
            rows = pl.ds(i * rg, rg)
            qrows = pl.ds(g * rg, rg)
            lg = _dot_nt(qlat_ref[0, h, qrows, :], c) + madd_ref[qrows, :]
            if with_bias:
                lg = lg + bias_ref[0, h, qrows, :]
            m_old = m_ref[rows, :]
            m_new = jnp.maximum(m_old, jnp.max(lg, axis=1, keepdims=True))
            alpha = jnp.exp(m_old - m_new)
            p = jnp.exp(lg - jnp.concatenate([m_new] * (tk // LANES), axis=1))
            m_ref[rows, :] = m_new
            acc_ref[rows, :] = (jnp.concatenate([alpha, alpha], axis=1) * acc_ref[rows, :]
                                + _dot(p.astype(BF16), c_one))

    @pl.when(kb >= first_near)
    def _():
        heads(True)

    @pl.when(kb < first_near)
    def _():
        heads(False)

    @pl.when(kb == first_near + qmul)
    def _():
        for j in range(nh // 2):
            r0, r1 = pl.ds(2 * j * tq, tq), pl.ds((2 * j + 1) * tq, tq)
            o = jnp.concatenate([acc_ref[r0, :KV_LATENT] / acc_ref[r0, KV_LATENT:],
                                 acc_ref[r1, :KV_LATENT] / acc_ref[r1, KV_LATENT:]], axis=1)
            out_ref[0, :, j * LANES:(j + 1) * LANES] = _dot(o.astype(BF16), wuv_ref[j])


def _dsa_attend(qlat, c, mask, bias, wuv2, tq, tk):
    b, nh, s, _ = qlat.shape
    nq = s // tq
    qmul = tq // tk
    rg = min(128, tq)
    qi_tab = np.concatenate([np.full(qmul * (q + 1), q, np.int32) for q in range(nq)])
    kb_tab = np.concatenate([np.arange(qmul * (q + 1), dtype=np.int32) for q in range(nq)])
    grid_spec = pltpu.PrefetchScalarGridSpec(
        num_scalar_prefetch=2,
        grid=(b, len(qi_tab)),
        in_specs=[pl.BlockSpec((1, nh, tq, KV_LATENT), lambda i, p, qt, kt: (i, 0, qt[p], 0)),
                  pl.BlockSpec((1, tk, c.shape[2]), lambda i, p, qt, kt: (i, kt[p], 0)),
                  pl.BlockSpec((1, tq, mask.shape[2]), lambda i, p, qt, kt: (i, qt[p], 0)),
                  pl.BlockSpec((1, nh, tq, tk),
                               lambda i, p, qt, kt: (jnp.clip(kt[p] - (qmul * qt[p] - 1), 0, qmul),
                                                     0, 0, 0)),
                  pl.BlockSpec(wuv2.shape, lambda i, p, qt, kt: (0, 0, 0))],
        out_specs=pl.BlockSpec((1, tq, nh * HEAD_DIM), lambda i, p, qt, kt: (i, qt[p], 0)),
        scratch_shapes=[pltpu.VMEM((nh * tq, LANES), F32),
                        pltpu.VMEM((nh * tq, 2 * KV_LATENT), F32),
                        pltpu.VMEM((tq, tk), F32)],
    )
    return pl.pallas_call(
        functools.partial(_dsa_attend_kernel, nh=nh, rg=rg),
        grid_spec=grid_spec,
        out_shape=jax.ShapeDtypeStruct((b, s, nh * HEAD_DIM), F32),
        compiler_params=_params("arbitrary", "arbitrary"),
        name="dsa_attend",
    )(jnp.asarray(qi_tab), jnp.asarray(kb_tab), qlat, c, mask, bias, wuv2)


def _t5_bucket(rel):
    n = jnp.maximum(rel, 0)
    max_exact = REL_BUCKETS // 2
    large = max_exact + (jnp.log(jnp.maximum(n, 1).astype(F32) / max_exact)
                         / math.log(REL_MAX_DIST / max_exact) * (REL_BUCKETS - max_exact)).astype(I32)
    large = jnp.minimum(large, REL_BUCKETS - 1)
    return jnp.where(n < max_exact, n, large)


def _bias_tiles(rel_bias, tq, tk):
    assert tk >= REL_MAX_DIST
    rel0 = jnp.arange(tq, dtype=I32)[:, None] - jnp.arange(tk, dtype=I32)[None, :]
    shifted = rel_bias - rel_bias[REL_BUCKETS - 1]
    buckets = jnp.stack([_t5_bucket(rel0 + (1 - j) * tk) for j in range(tq // tk + 1)])
    onehot = jax.nn.one_hot(buckets, REL_BUCKETS, dtype=F32)
    return jnp.einsum('dijb,bh->dhij', onehot, shifted, precision=lax.Precision.HIGHEST)


def _post_kernel(x_ref, tok_ref, mq_ref, kv_ref, wout_ref, g_ref, b_ref, x1_ref, *, tok_w):
    mq = mq_ref[0]
    kv = kv_ref[0]
    y = _dot(tok_ref[0].astype(BF16), wout_ref[:tok_w, :])
    for h in range(MEM_HEADS):
        q = mq[:, h * HEAD_DIM:(h + 1) * HEAD_DIM].astype(BF16)
        k = kv[:, h * HEAD_DIM:(h + 1) * HEAD_DIM]
        v = kv[:, MEM_WIDTH + h * HEAD_DIM:MEM_WIDTH + (h + 1) * HEAD_DIM]
        lg = _dot_nt(q, k) * HEAD_DIM ** -0.5
        p = jnp.exp(lg - jnp.max(lg, axis=1, keepdims=True)).astype(BF16)
        o = _dot(p, v) / _dot(p, jnp.ones((p.shape[1], LANES), BF16))[:, :HEAD_DIM]
        y = y + _dot(o.astype(BF16), wout_ref[tok_w + h * HEAD_DIM:tok_w + (h + 1) * HEAD_DIM, :])
    x1_ref[0] = _layer_norm(DN_ALPHA * x_ref[0] + y, g_ref[...], b_ref[...])


def _post_mixer(x, tok, mq, kv, w_out, g, b, ts):
    bsz, s, d = x.shape
    tok_w = tok.shape[2]
    const = lambda shape: pl.BlockSpec(shape, lambda i, j: (0,) * len(shape))
    tile = lambda w_: pl.BlockSpec((1, ts, w_), lambda i, j: (i, j, 0))
    return pl.pallas_call(
        functools.partial(_post_kernel, tok_w=tok_w),
        grid=(bsz, s // ts),
        in_specs=[tile(d), tile(tok_w), tile(mq.shape[2]),
                  pl.BlockSpec((1,) + kv.shape[1:], lambda i, j: (i, 0, 0)),
                  const(w_out.shape), const(g.shape), const(b.shape)],
        out_specs=tile(d),
        out_shape=jax.ShapeDtypeStruct((bsz, s, d), F32),
        compiler_params=_params("arbitrary", "arbitrary"),
        name="post_mixer",
    )(x, tok, mq, kv, w_out, g, b)


def _router_kernel(x_ref, wh_ref, wl_ref, rb_ref, e_ref, gate_ref, rank_ref, cnt_ref, carry_ref):
    @pl.when(pl.program_id(0) == 0)
    def _():
        carry_ref[...] = jnp.zeros_like(carry_ref)

    x = x_ref[...]
    tr = x.shape[0]
    xh = x.astype(BF16)
    xl = (x - xh.astype(F32)).astype(BF16)
    lg = _dot(xh, wh_ref[...]) + (_dot(xl, wh_ref[...]) + _dot(xh, wl_ref[...])) + rb_ref[...]
    lane = lax.broadcasted_iota(I32, lg.shape, 1).astype(F32)
    lg = jnp.where(lane < N_EXPERTS, lg, -jnp.inf)

    idx, val = [], []
    onehot = jnp.zeros(lg.shape, F32)
    for _ in range(TOP_K):
        m = jnp.max(lg, axis=1, keepdims=True)
        i = jnp.min(jnp.where(lg == m, lane, float(LANES)), axis=1, keepdims=True)
        hit = lane == i
        idx.append(i)
        val.append(m)
        onehot = jnp.where(hit, 1.0, onehot)
        lg = jnp.where(hit, -jnp.inf, lg)

    ex = [jnp.exp(v - val[0]) for v in val]
    den = ex[0] + ex[1] + ex[2] + ex[3]

    r_i = lax.broadcasted_iota(I32, (tr, tr), 0)
    c_i = lax.broadcasted_iota(I32, (tr, tr), 1)
    tri = jnp.where(c_i < r_i, 1.0, 0.0).astype(BF16)
    before = _dot(tri, onehot.astype(BF16)) + carry_ref[...]
    carry_ref[...] = carry_ref[...] + jnp.sum(onehot, axis=0, keepdims=True)
    cnt_ref[...] = carry_ref[...]

    lane_k = lax.broadcasted_iota(I32, (tr, TOP_K), 1)
    e_out = jnp.zeros((tr, TOP_K), I32)
    g_out = jnp.zeros((tr, TOP_K), F32)
    r_out = jnp.zeros((tr, TOP_K), I32)
    for k in range(TOP_K):
        rk = jnp.sum(jnp.where(lane == idx[k], before, 0.0), axis=1, keepdims=True).astype(I32)
        e_out = jnp.where(lane_k == k, idx[k].astype(I32), e_out)
        g_out = jnp.where(lane_k == k, ex[k] / den, g_out)
        r_out = jnp.where(lane_k == k, rk, r_out)
    e_ref[...] = e_out
    gate_ref[...] = g_out
    rank_ref[...] = r_out


def _router(x, wh, wl, rb, tr):
    t, d = x.shape
    const = lambda shape: pl.BlockSpec(shape, lambda i: (0,) * len(shape))
    small = pl.BlockSpec((tr, TOP_K), lambda i: (i, 0))
    return pl.pallas_call(
        _router_kernel,
        grid=(t // tr,),
        in_specs=[pl.BlockSpec((tr, d), lambda i: (i, 0)), const(wh.shape), const(wl.shape),
                  const(rb.shape)],
        out_specs=[small, small, small, const((1, LANES))],
        out_shape=[jax.ShapeDtypeStruct((t, TOP_K), I32), jax.ShapeDtypeStruct((t, TOP_K), F32),
                   jax.ShapeDtypeStruct((t, TOP_K), I32), jax.ShapeDtypeStruct((1, LANES), F32)],
        scratch_shapes=[pltpu.VMEM((1, LANES), F32)],
        compiler_params=_params("arbitrary"),
        name="router",
    )(x, wh, wl, rb)


def _dest_kernel(e_ref, rank_ref, start_ref, dest_ref):
    e = e_ref[...]
    lane = lax.broadcasted_iota(I32, (e.shape[0], LANES), 1)
    lane_k = lax.broadcasted_iota(I32, e.shape, 1)
    out = rank_ref[...]
    for k in range(TOP_K):
        st = jnp.sum(jnp.where(lane == e[:, k:k + 1], start_ref[...], 0.0), axis=1, keepdims=True)
        out = out + jnp.where(lane_k == k, st.astype(I32), 0)
    dest_ref[...] = out


def _dest(e, rank, start, tr):
    t = e.shape[0]
    small = pl.BlockSpec((tr, TOP_K), lambda i: (i, 0))
    return pl.pallas_call(
        _dest_kernel,
        grid=(t // tr,),
        in_specs=[small, small, pl.BlockSpec((1, LANES), lambda i: (0, 0))],
        out_specs=small,
        out_shape=jax.ShapeDtypeStruct((t, TOP_K), I32),
        compiler_params=_params("arbitrary"),
        name="dest",
    )(e, rank, start)


def _dispatch_kernel(dest_ref, x_ref, xs_in_ref, xs_ref, sem):
    del xs_in_ref
    tt = x_ref.shape[0]

    def row_copy(i, k):
        d = dest_ref[i * TOP_K + k]
        return pltpu.make_async_copy(x_ref.at[pl.ds(i, 1)], xs_ref.at[pl.ds(d, 1)], sem)

    def issue(i, _):
        for k in range(TOP_K):
            row_copy(i, k).start()
        return 0

    def drain(i, _):
        for k in range(TOP_K):
            row_copy(i, k).wait()
        return 0

    lax.fori_loop(0, tt, issue, 0, unroll=8)
    lax.fori_loop(0, tt, drain, 0, unroll=8)


def _dispatch(dest_flat, x, rows, tt):
    t, d = x.shape
    xs0 = jnp.zeros((rows, d), x.dtype)
    return pl.pallas_call(
        _dispatch_kernel,
        grid=(t // tt,),
        in_specs=[pl.BlockSpec((tt * TOP_K,), lambda i: (i,), memory_space=pltpu.SMEM),
                  pl.BlockSpec((tt, d), lambda i: (i, 0)),
                  pl.BlockSpec(memory_space=pl.ANY)],
        out_specs=pl.BlockSpec(memory_space=pl.ANY),
        out_shape=jax.ShapeDtypeStruct((rows, d), x.dtype),
        scratch_shapes=[pltpu.SemaphoreType.DMA(())],
        input_output_aliases={2: 0},
        compiler_params=_params("arbitrary"),
        name="dispatch",
    )(dest_flat, x, xs0)


def _ffn_kernel(be_ref, xs_ref, w1_ref, b1_ref, w2_ref, b2_ref, ys_ref):
    del be_ref
    de = w2_ref.shape[1]
    h = _dot(xs_ref[...].astype(BF16), w1_ref[0]) + b1_ref[0]
    gt = jnp.minimum(h[:, :de], SWIGLU_LIMIT)
    up = jnp.clip(h[:, de:], -SWIGLU_LIMIT, SWIGLU_LIMIT)
    act = (up + 1.0) * (gt * _sigmoid(SWIGLU_ALPHA * gt))
    ys_ref[...] = _dot(act.astype(BF16), w2_ref[0]) + b2_ref[0]


def _expert_ffn(blk_e, xs, w1, b1, w2, b2, tm):
    rows, d = xs.shape
    grid_spec = pltpu.PrefetchScalarGridSpec(
        num_scalar_prefetch=1,
        grid=(rows // tm,),
        in_specs=[pl.BlockSpec((tm, d), lambda i, be: (i, 0)),
                  pl.BlockSpec((1,) + w1.shape[1:], lambda i, be: (be[i], 0, 0)),
                  pl.BlockSpec((1,) + b1.shape[1:], lambda i, be: (be[i], 0, 0)),
                  pl.BlockSpec((1,) + w2.shape[1:], lambda i, be: (be[i], 0, 0)),
                  pl.BlockSpec((1,) + b2.shape[1:], lambda i, be: (be[i], 0, 0))],
        out_specs=pl.BlockSpec((tm, d), lambda i, be: (i, 0)),
    )
    return pl.pallas_call(
        _ffn_kernel,
        grid_spec=grid_spec,
        out_shape=jax.ShapeDtypeStruct((rows, d), F32),
        compiler_params=_params("arbitrary"),
        name="expert_ffn",
    )(blk_e, xs, w1, b1, w2, b2)


def _combine_kernel(dest_ref, x_ref, gate_ref, g_ref, b_ref, ys_ref, out_ref, buf, sem):
    tt = x_ref.shape[0]

    def row_copy(i, k):
        d = dest_ref[i * TOP_K + k]
        return pltpu.make_async_copy(ys_ref.at[pl.ds(d, 1)], buf.at[k, pl.ds(i, 1)], sem)

    def issue(i, _):
        for k in range(TOP_K):
            row_copy(i, k).start()
        return 0

    def drain(i, _):
        for k in range(TOP_K):
            row_copy(i, k).wait()
        return 0

    lax.fori_loop(0, tt, issue, 0, unroll=8)
    lax.fori_loop(0, tt, drain, 0, unroll=8)

    gate = gate_ref[...]
    ffn = gate[:, 0:1] * buf[0]
    for k in range(1, TOP_K):
        ffn = ffn + gate[:, k:k + 1] * buf[k]
    out_ref[...] = _layer_norm(DN_ALPHA * x_ref[...] + ffn, g_ref[...], b_ref[...])


def _combine(dest_flat, x, gate, g, b, ys, tt):
    t, d = x.shape
    const = lambda shape: pl.BlockSpec(shape, lambda i: (0,) * len(shape))
    return pl.pallas_call(
        _combine_kernel,
        grid=(t // tt,),
        in_specs=[pl.BlockSpec((tt * TOP_K,), lambda i: (i,), memory_space=pltpu.SMEM),
                  pl.BlockSpec((tt, d), lambda i: (i, 0)),
                  pl.BlockSpec((tt, TOP_K), lambda i: (i, 0)),
                  const(g.shape), const(b.shape),
                  pl.BlockSpec(memory_space=pl.ANY)],
        out_specs=pl.BlockSpec((tt, d), lambda i: (i, 0)),
        out_shape=jax.ShapeDtypeStruct((t, d), F32),
        scratch_shapes=[pltpu.VMEM((TOP_K, tt, d), F32), pltpu.SemaphoreType.DMA(())],
        compiler_params=_params("arbitrary"),
        name="combine",
    )(dest_flat, x, gate, g, b, ys)


def _moe_layer(x, router_w, router_b, w1, b1, w2, b2, g, b, tiles):
    t, d = x.shape
    tm, tt, tr = tiles["tm"], tiles["tt"], tiles["tr"]
    wpad = jnp.pad(router_w, ((0, 0), (0, LANES - N_EXPERTS)))
    wh = wpad.astype(BF16)
    wl = (wpad - wh.astype(F32)).astype(BF16)
    rb = jnp.pad(router_b, (0, LANES - N_EXPERTS))[None, :]
    e, gate, rank, cnt = _router(x, wh, wl, rb, tr)

    counts = cnt[0, :N_EXPERTS].astype(I32)
    padded = (counts + tm - 1) // tm * tm
    pend = jnp.cumsum(padded)
    start = jnp.pad(pend - padded, (0, LANES - N_EXPERTS))[None, :].astype(F32)
    ntile = t * TOP_K // tm + N_EXPERTS
    tile_row = jnp.arange(ntile, dtype=I32) * tm
    blk_e = jnp.minimum(jnp.sum(pend[None, :] <= tile_row[:, None], axis=1), N_EXPERTS - 1).astype(I32)

    dest = _dest(e, rank, start, tr).reshape(-1)
    xs = _dispatch(dest, x, ntile * tm, tt)
    ys = _expert_ffn(blk_e, xs, w1.astype(BF16), b1[:, None, :], w2.astype(BF16), b2[:, None, :], tm)
    return _combine(dest, x, gate, g[None, :], b[None, :], ys, tt)


def _block_diag(w):
    n, c, _ = w.shape
    eye = jnp.eye(n, dtype=w.dtype)
    return (eye[:, None, :, None] * w[:, :, None, :]).reshape(n * c, n * c)


def kernel(x, mem, rel_bias, a_w_in, a_conv_w, a_conv_b, a_wr, a_br, a_wi, a_bi, a_lambda, b_w_in, b_kv_norm_g, b_w_uk, b_w_uv, b_idx_norm_g, b_idx_norm_b, w_mem_kv, w_out, ln1_g, ln1_b, router_w, router_b, exp_w1, exp_b1, exp_w2, exp_b2, ln2_g, ln2_b):
    bsz, seq, d = x.shape
    tiles = _tiles(seq)
    ts = tiles["ts"]
    row = lambda v: v[None, :]

    def finish_layer(layer, x, tok, mq):
        kv = _matmul(mem.reshape(-1, d), w_mem_kv[layer].astype(BF16), mem.shape[1])
        kv = kv.astype(BF16).reshape(bsz, mem.shape[1], -1)
        x1 = _post_mixer(x, tok, mq, kv, w_out[layer].astype(BF16), row(ln1_g[layer]),
                         row(ln1_b[layer]), ts)
        x2 = _moe_layer(x1.reshape(-1, d), router_w[layer], router_b[layer], exp_w1[layer],
                        exp_b1[layer], exp_w2[layer], exp_b2[layer], ln2_g[layer], ln2_b[layer],
                        tiles)
        return x2.reshape(bsz, seq, d)

    tok, mq = _rglru_front(x, a_w_in[0].astype(BF16), a_conv_w[0], row(a_conv_b[0]),
                           _block_diag(a_wr[0]).astype(BF16), row(a_br[0]),
                           _block_diag(a_wi[0]).astype(BF16), row(a_bi[0]), row(a_lambda[0]), ts)
    x = finish_layer(0, x, tok, mq)

    nh = b_w_uk.shape[2]
    tok_w = nh * HEAD_DIM
    q0, c0, iq0, mq0, ik0, iw0, total = _dsa_layout(tok_w)
    w = b_w_in[0]
    o_c, o_iq = tok_w, tok_w + KV_LATENT
    o_ik = o_iq + IDX_HEADS * IDX_DIM
    o_iw = o_ik + IDX_DIM
    o_mq = o_iw + IDX_HEADS
    w_re = jnp.zeros((d, total), F32)
    w_re = w_re.at[:, q0:q0 + tok_w].set(w[:, :tok_w])
    w_re = w_re.at[:, c0:c0 + KV_LATENT].set(w[:, o_c:o_iq])
    w_re = w_re.at[:, iq0:iq0 + IDX_HEADS * IDX_DIM].set(w[:, o_iq:o_ik])
    w_re = w_re.at[:, ik0:ik0 + IDX_DIM].set(w[:, o_ik:o_iw])
    w_re = w_re.at[:, iw0:iw0 + IDX_HEADS].set(w[:, o_iw:o_mq])
    w_re = w_re.at[:, mq0:mq0 + MEM_WIDTH].set(w[:, o_mq:])
    wuk = jnp.transpose(b_w_uk[0], (1, 2, 0)).astype(BF16)
    wuv = jnp.transpose(b_w_uv[0], (1, 0, 2))
    wuv2 = jnp.zeros((nh // 2, 2 * KV_LATENT, 2 * HEAD_DIM), F32)
    wuv2 = wuv2.at[:, :KV_LATENT, :HEAD_DIM].set(wuv[0::2])
    wuv2 = wuv2.at[:, KV_LATENT:, HEAD_DIM:].set(wuv[1::2]).astype(BF16)
    qlat, c, iq, ik, iw, mq = _dsa_proj(x, w_re.astype(BF16), wuk, row(b_kv_norm_g[0]),
                                        row(b_idx_norm_g[0]), row(b_idx_norm_b[0]), ts)
    k_sel = min(TOPK_MAX, seq // 4)
    chunk = tiles["chunk"]
    ikt = jnp.swapaxes(ik.reshape(bsz, MASK_BITS, chunk, IDX_DIM), 2, 3)
    mask = _dsa_select(iq, iw, ikt, chunk, k_sel)
    tok = _dsa_attend(qlat, c, mask, _bias_tiles(rel_bias, tiles["tq_att"], chunk), wuv2,
                      tiles["tq_att"], chunk)
    x = finish_layer(1, x, tok, mq)
    return x
```

```python
import functools
import math

import jax
import jax.numpy as jnp
import numpy as np
from jax import lax
from jax.experimental import pallas as pl
from jax.experimental.pallas import tpu as pltpu

F32 = jnp.float32
BF16 = jnp.bfloat16
I32 = jnp.int32

HEAD_DIM = 64
MEM_HEADS = 4
MEM_WIDTH = MEM_HEADS * HEAD_DIM
LRU_C = 8.0
CONV_W = 4
KV_LATENT = 128
IDX_HEADS = 4
IDX_DIM = 64
TOPK_MAX = 256
REL_BUCKETS = 32
REL_MAX_DIST = 128
N_EXPERTS = 32
TOP_K = 4
SWIGLU_LIMIT = 7.0
SWIGLU_ALPHA = 1.702
DEPTH = 2
DN_ALPHA = (2 * DEPTH) ** 0.25
LN_EPS = 1e-5
RMS_EPS = 1e-6

SUBLANES = 8
LANES = 128
MASK_BITS = 32
INT_MIN = -2 ** 31
NEG_BIG = -1e30
VMEM_LIMIT = 56 * 1024 * 1024


def _tiles(seq):
    chunk = seq // MASK_BITS
    return dict(
        ts=min(512, seq),
        chunk=chunk,
        tq_att=2 * chunk,
        tt=256,
        tm=512,
        tr=512,
    )


def _dot(a, b):
    return jnp.dot(a, b, preferred_element_type=F32)


def _dot_nt(a, b):
    return lax.dot_general(a, b, (((1,), (1,)), ((), ())), preferred_element_type=F32)


def _sigmoid(x):
    return 1.0 / (1.0 + jnp.exp(-x))


def _layer_norm(z, g, b):
    mu = jnp.mean(z, axis=-1, keepdims=True)
    zc = z - mu
    var = jnp.mean(zc * zc, axis=-1, keepdims=True)
    return zc * lax.rsqrt(var + LN_EPS) * g + b


def _params(*sem):
    return pltpu.CompilerParams(dimension_semantics=sem, vmem_limit_bytes=VMEM_LIMIT)


def _mm_kernel(a_ref, b_ref, o_ref):
    o_ref[...] = _dot(a_ref[...].astype(BF16), b_ref[...])


def _matmul(a, b, tm):
    m, k = a.shape
    n = b.shape[1]
    return pl.pallas_call(
        _mm_kernel,
        grid=(m // tm,),
        in_specs=[pl.BlockSpec((tm, k), lambda i: (i, 0)),
                  pl.BlockSpec((k, n), lambda i: (0, 0))],
        out_specs=pl.BlockSpec((tm, n), lambda i: (i, 0)),
        out_shape=jax.ShapeDtypeStruct((m, n), F32),
        compiler_params=_params("arbitrary"),
        name="matmul",
    )(a, b)


def _rglru_kernel(x_ref, win_ref, cw_ref, cb_ref, wr_ref, br_ref, wi_ref, bi_ref, lam_ref,
                  tok_ref, mq_ref, tail_ref, h_ref, a_s, u_s, *, tok_w):
    s = pl.program_id(1)

    @pl.when(s == 0)
    def _():
        tail_ref[...] = jnp.zeros_like(tail_ref)
        h_ref[...] = jnp.zeros_like(h_ref)

    proj = _dot(x_ref[0].astype(BF16), win_ref[...])
    xb = proj[:, :tok_w]
    gb = proj[:, tok_w:2 * tok_w]
    mq_ref[0] = proj[:, 2 * tok_w:]
    ts = xb.shape[0]

    xe = jnp.concatenate([tail_ref[...], xb], axis=0)
    tail_ref[...] = xb[ts - SUBLANES:, :]
    xc = xb * cw_ref[CONV_W - 1:CONV_W, :] + cb_ref[...]
    for d in range(1, CONV_W):
        sh = pltpu.roll(xe, d, 0)[SUBLANES:, :]
        xc = xc + sh * cw_ref[CONV_W - 1 - d:CONV_W - d, :]

    xcb = xc.astype(BF16)
    r = _sigmoid(_dot(xcb, wr_ref[...]) + br_ref[...])
    ig = _sigmoid(_dot(xcb, wi_ref[...]) + bi_ref[...])
    nl = -lam_ref[...]
    softplus = jnp.maximum(nl, 0.0) + jnp.log1p(jnp.exp(-jnp.abs(nl)))
    log_a = (-LRU_C) * r * softplus
    a = jnp.exp(log_a)
    th = jnp.tanh(log_a)
    u = jnp.sqrt(-2.0 * th / (1.0 - th)) * (ig * xc)

    row = lax.broadcasted_iota(I32, a.shape, 0) & (SUBLANES - 1)
    for d in (1, 2, 4):
        m = row >= d
        u = jnp.where(m, a * pltpu.roll(u, d, 0) + u, u)
        a = jnp.where(m, a * pltpu.roll(a, d, 0), a)
    a_s[...] = a
    u_s[...] = u

    def group(g, h):
        sl = pl.ds(pl.multiple_of(g * SUBLANES, SUBLANES), SUBLANES)
        hg = u_s[sl, :] + a_s[sl, :] * h
        u_s[sl, :] = hg
        return jnp.broadcast_to(hg[SUBLANES - 1:, :], hg.shape)

    h_ref[...] = lax.fori_loop(0, ts // SUBLANES, group, h_ref[...])

    gelu = 0.5 * gb * (1.0 + jnp.tanh(math.sqrt(2.0 / math.pi) * (gb + 0.044715 * (gb * gb * gb))))
    tok_ref[0] = u_s[...] * gelu


def _rglru_front(x, w_in, conv_w, conv_b, wr, br, wi, bi, lam, ts):
    b, s, d = x.shape
    tok_w = conv_w.shape[-1]
    mem_w = w_in.shape[-1] - 2 * tok_w
    const = lambda shape: pl.BlockSpec(shape, lambda i, j: (0,) * len(shape))
    return pl.pallas_call(
        functools.partial(_rglru_kernel, tok_w=tok_w),
        grid=(b, s // ts),
        in_specs=[pl.BlockSpec((1, ts, d), lambda i, j: (i, j, 0)),
                  const(w_in.shape), const(conv_w.shape), const(conv_b.shape),
                  const(wr.shape), const(br.shape), const(wi.shape), const(bi.shape),
                  const(lam.shape)],
        out_specs=[pl.BlockSpec((1, ts, tok_w), lambda i, j: (i, j, 0)),
                   pl.BlockSpec((1, ts, mem_w), lambda i, j: (i, j, 0))],
        out_shape=[jax.ShapeDtypeStruct((b, s, tok_w), F32),
                   jax.ShapeDtypeStruct((b, s, mem_w), F32)],
        scratch_shapes=[pltpu.VMEM((SUBLANES, tok_w), F32), pltpu.VMEM((SUBLANES, tok_w), F32),
                        pltpu.VMEM((ts, tok_w), F32), pltpu.VMEM((ts, tok_w), F32)],
        compiler_params=_params("arbitrary", "arbitrary"),
        name="rglru_front",
    )(x, w_in, conv_w, conv_b, wr, br, wi, bi, lam)


def _dsa_layout(tok_w):
    q0 = 0
    c0 = q0 + tok_w
    iq0 = c0 + KV_LATENT
    mq0 = iq0 + IDX_HEADS * IDX_DIM
    ik0 = mq0 + MEM_WIDTH
    iw0 = ik0 + LANES
    total = iw0 + LANES
    return q0, c0, iq0, mq0, ik0, iw0, total


def _dsa_proj_kernel(x_ref, w_ref, wuk_ref, kvg_ref, ig_ref, ib_ref,
                     qlat_ref, c_ref, iq_ref, ik_ref, iw_ref, mq_ref, *, tok_w):
    q0, c0, iq0, mq0, ik0, iw0, _ = _dsa_layout(tok_w)
    proj = _dot(x_ref[0].astype(BF16), w_ref[...])
    for h in range(tok_w // HEAD_DIM):
        qh = proj[:, q0 + h * HEAD_DIM:q0 + (h + 1) * HEAD_DIM].astype(BF16)
        ql = _dot(qh, wuk_ref[h]) * HEAD_DIM ** -0.5
        qlat_ref[0, h] = ql.astype(BF16)
    c = proj[:, c0:c0 + KV_LATENT]
    c = c * lax.rsqrt(jnp.mean(c * c, axis=-1, keepdims=True) + RMS_EPS) * kvg_ref[...]
    c_ref[0] = c.astype(BF16)
    iq_ref[0] = proj[:, iq0:iq0 + IDX_HEADS * IDX_DIM].astype(BF16)
    ik = _layer_norm(proj[:, ik0:ik0 + IDX_DIM], ig_ref[...], ib_ref[...])
    ik_ref[0] = ik.astype(BF16)
    iw_ref[0] = proj[:, iw0:iw0 + IDX_HEADS] * (IDX_HEADS ** -0.5 * IDX_DIM ** -0.5)
    mq_ref[0] = proj[:, mq0:mq0 + MEM_WIDTH]


def _dsa_proj(x, w, wuk, kvg, ig, ib, ts):
    b, s, d = x.shape
    nh = wuk.shape[0]
    tok_w = nh * HEAD_DIM
    const = lambda shape: pl.BlockSpec(shape, lambda i, j: (0,) * len(shape))
    tile = lambda w_: pl.BlockSpec((1, ts, w_), lambda i, j: (i, j, 0))
    widths = [KV_LATENT, IDX_HEADS * IDX_DIM, IDX_DIM, IDX_HEADS, MEM_WIDTH]
    dtypes = [BF16, BF16, BF16, F32, F32]
    return pl.pallas_call(
        functools.partial(_dsa_proj_kernel, tok_w=tok_w),
        grid=(b, s // ts),
        in_specs=[tile(d), const(w.shape), const(wuk.shape), const(kvg.shape),
                  const(ig.shape), const(ib.shape)],
        out_specs=[pl.BlockSpec((1, nh, ts, KV_LATENT), lambda i, j: (i, 0, j, 0))]
        + [tile(w_) for w_ in widths],
        out_shape=[jax.ShapeDtypeStruct((b, nh, s, KV_LATENT), BF16)]
        + [jax.ShapeDtypeStruct((b, s, w_), t) for w_, t in zip(widths, dtypes)],
        compiler_params=_params("arbitrary", "arbitrary"),
        name="dsa_proj",
    )(x, w, wuk, kvg, ig, ib)


def _row_popcount(words):
    pc = lax.population_count(words)
    part = pc[:, :LANES]
    for i in range(1, words.shape[1] // LANES):
        part = part + pc[:, i * LANES:(i + 1) * LANES]
    return jnp.sum(part.astype(F32), axis=1, keepdims=True)


def _dsa_select_kernel(iq_ref, iw_ref, ikt_ref, mask_ref, keys_ref, iwb_ref, tie_ref,
                       *, k_sel, chunk):
    qi = pl.program_id(1)
    tq = iq_ref.shape[1]
    iq = iq_ref[0]
    iw = iw_ref[0]
    for h in range(IDX_HEADS):
        iwb_ref[h] = jnp.broadcast_to(iw[:, h:h + 1], (tq, chunk))
    iqh = [iq[:, h * IDX_DIM:(h + 1) * IDX_DIM] for h in range(IDX_HEADS)]

    def ukey(c):
        ikc = ikt_ref[0, c]
        sc = jnp.maximum(_dot(iqh[0], ikc), 0.0) * iwb_ref[0]
        for h in range(1, IDX_HEADS):
            sc = sc + jnp.maximum(_dot(iqh[h], ikc), 0.0) * iwb_ref[h]
        bits = lax.bitcast_convert_type(sc, I32)
        word = bits ^ ((bits >> 31) | INT_MIN)
        return word + (word == 0x7FFFFFFF).astype(I32)

    def score(c, _):
        keys_ref[c] = ukey(c)
        return 0

    lax.fori_loop(0, qi, score, 0)
    row = lax.broadcasted_iota(I32, (tq, chunk), 0)
    lane = lax.broadcasted_iota(I32, (tq, chunk), 1)
    keys_ref[qi] = jnp.where(lane <= row, ukey(qi), 0)

    def clear(c, _):
        keys_ref[c] = jnp.zeros((tq, chunk), I32)
        return 0

    lax.fori_loop(qi + 1, MASK_BITS, clear, 0)

    def transpose(g, _):
        rows = pl.ds(pl.multiple_of(g * SUBLANES, SUBLANES), SUBLANES)
        for half in range(chunk // LANES):
            cols = slice(half * LANES, (half + 1) * LANES)
            a = [keys_ref[c, rows, cols] for c in range(MASK_BITS)]
            j, m = MASK_BITS // 2, 0x0000FFFF
            while j:
                k = 0
                while k < MASK_BITS:
                    t = (lax.shift_right_logical(a[k], j) ^ a[k + j]) & m
                    a[k] = a[k] ^ (t << j)
                    a[k + j] = a[k + j] ^ t
                    k = (k + j + 1) & ~j
                j >>= 1
                m = (m ^ (m << j)) & 0xFFFFFFFF
                m = m - (1 << 32) if m >= (1 << 31) else m
            for c in range(MASK_BITS):
                keys_ref[c, rows, cols] = a[c]
        return 0

    lax.fori_loop(0, tq // SUBLANES, transpose, 0)

    def plane_step(i, carry):
        active, greater, need, any_one = carry
        ones = active & keys_ref[MASK_BITS - 1 - i]
        cnt = _row_popcount(ones)
        take = cnt >= need
        active = jnp.where(take, ones, active ^ ones)
        greater = jnp.where(take, greater, greater | ones)
        need = jnp.where(take, need, need - cnt)
        return active, greater, need, jnp.where(take, 1.0, any_one)

    full = jnp.full((tq, chunk), -1, I32)
    ties, greater, need, any_one = lax.fori_loop(
        0, MASK_BITS, plane_step,
        (full, jnp.zeros((tq, chunk), I32), jnp.full((tq, 1), float(k_sel), F32),
         jnp.zeros((tq, 1), F32)))
    ties = jnp.where(any_one > 0.0, ties, 0)
    tie_ref[...] = ties

    @pl.when(jnp.max(_row_popcount(ties) - need) > 0.0)
    def _():
        shift = chunk.bit_length() - 1
        lane_i = lax.broadcasted_iota(I32, (tq, chunk), 1)

        def below(v, inclusive):
            vh = jnp.broadcast_to(v >> shift, (tq, chunk))
            vl = jnp.broadcast_to(v & (chunk - 1), (tq, chunk))
            top = jnp.int32(1) << vh
            edge = (lane_i <= vl) if inclusive else (lane_i < vl)
            return (top - 1) | jnp.where(edge, top, 0)

        nbits = (MASK_BITS * chunk).bit_length() - 1

        def idx_step(i, j):
            cand = j | (jnp.int32(1) << (nbits - 1 - i))
            cnt = _row_popcount(ties & below(cand, False))
            return jnp.where(cnt < need, cand, j)

        last = lax.fori_loop(0, nbits, idx_step, jnp.zeros((tq, 1), I32))
        tie_ref[...] = ties & below(last, True)

    mask_ref[0] = greater | tie_ref[...]


def _dsa_select(iq, iw, ikt, chunk, k_sel):
    b, s, _ = iq.shape
    tq = chunk
    return pl.pallas_call(
        functools.partial(_dsa_select_kernel, k_sel=k_sel, chunk=chunk),
        grid=(b, s // tq),
        in_specs=[pl.BlockSpec((1, tq, iq.shape[2]), lambda i, j: (i, j, 0)),
                  pl.BlockSpec((1, tq, iw.shape[2]), lambda i, j: (i, j, 0)),
                  pl.BlockSpec((1,) + ikt.shape[1:], lambda i, j: (i, 0, 0, 0))],
        out_specs=pl.BlockSpec((1, tq, chunk), lambda i, j: (i, j, 0)),
        out_shape=jax.ShapeDtypeStruct((b, s, chunk), I32),
        scratch_shapes=[pltpu.VMEM((MASK_BITS, tq, chunk), I32),
                        pltpu.VMEM((IDX_HEADS, tq, chunk), F32),
                        pltpu.VMEM((tq, chunk), I32)],
        compiler_params=_params("arbitrary", "arbitrary"),
        name="dsa_select",
    )(iq, iw, ikt)


def _dsa_attend_kernel(qi_ref, kb_ref, qlat_ref, c_ref, mask_ref, bias_ref, wuv_ref, out_ref,
                       m_ref, acc_ref, madd_ref, *, nh, rg):
    p_id = pl.program_id(1)
    qi = qi_ref[p_id]
    kb = kb_ref[p_id]
    tq, tk = madd_ref.shape
    ngroups = tq // rg
    qmul = tq // tk
    first_near = qmul * qi - 1

    @pl.when(kb == 0)
    def _():
        m_ref[...] = jnp.full(m_ref.shape, NEG_BIG, F32)
        acc_ref[...] = jnp.zeros_like(acc_ref)

    madd_ref[...] = jnp.where(((mask_ref[0] >> kb) & 1) == 1, 0.0, NEG_BIG)
    c = c_ref[0]
    c_one = jnp.concatenate([c, jnp.ones_like(c)], axis=1)

    def heads(with_bias):
        for i in range(nh * ngroups):
            h, g = divmod(i, ngroups)
            rows = pl.ds(i * rg, rg)
            qrows = pl.ds(g * rg, rg)
            lg = _dot_nt(qlat_ref[0, h, qrows, :], c) + madd_ref[qrows, :]
            if with_bias:
                lg = lg + bias_ref[0, h, qrows, :]
            m_old = m_ref[rows, :]
            m_new = jnp.maximum(m_old, jnp.max(lg, axis=1, keepdims=True))
            alpha = jnp.exp(m_old - m_new)
            p = jnp.exp(lg - jnp.concatenate([m_new] * (tk // LANES), axis=1))
            m_ref[rows, :] = m_new
            acc_ref[rows, :] = (jnp.concatenate([alpha, alpha], axis=1) * acc_ref[rows, :]
                                + _dot(p.astype(BF16), c_one))

    @pl.when(kb >= first_near)
    def _():
        heads(True)

    @pl.when(kb < first_near)
    def _():
        heads(False)

    @pl.when(kb == first_near + qmul)
    def _():
        for j in range(nh // 2):
            r0, r1 = pl.ds(2 * j * tq, tq), pl.ds((2 * j + 1) * tq, tq)
            o = jnp.concatenate([acc_ref[r0, :KV_LATENT] / acc_ref[r0, KV_LATENT:],
                                 acc_ref[r1, :KV_LATENT] / acc_ref[r1, KV_LATENT:]], axis=1)
            out_ref[0, :, j * LANES:(j + 1) * LANES] = _dot(o.astype(BF16), wuv_ref[j])


def _dsa_attend(qlat, c, mask, bias, wuv2, tq, tk):
    b, nh, s, _ = qlat.shape
    nq = s // tq
    qmul = tq // tk
    rg = min(128, tq)
    qi_tab = np.concatenate([np.full(qmul * (q + 1), q, np.int32) for q in range(nq)])
    kb_tab = np.concatenate([np.arange(qmul * (q + 1), dtype=np.int32) for q in range(nq)])
    grid_spec = pltpu.PrefetchScalarGridSpec(
        num_scalar_prefetch=2,
        grid=(b, len(qi_tab)),
        in_specs=[pl.BlockSpec((1, nh, tq, KV_LATENT), lambda i, p, qt, kt: (i, 0, qt[p], 0)),
                  pl.BlockSpec((1, tk, c.shape[2]), lambda i, p, qt, kt: (i, kt[p], 0)),
                  pl.BlockSpec((1, tq, mask.shape[2]), lambda i, p, qt, kt: (i, qt[p], 0)),
                  pl.BlockSpec((1, nh, tq, tk),
                               lambda i, p, qt, kt: (jnp.clip(kt[p] - (qmul * qt[p] - 1), 0, qmul),
                                                     0, 0, 0)),
                  pl.BlockSpec(wuv2.shape, lambda i, p, qt, kt: (0, 0, 0))],
        out_specs=pl.BlockSpec((1, tq, nh * HEAD_DIM), lambda i, p, qt, kt: (i, qt[p], 0)),
        scratch_shapes=[pltpu.VMEM((nh * tq, LANES), F32),
                        pltpu.VMEM((nh * tq, 2 * KV_LATENT), F32),
                        pltpu.VMEM((tq, tk), F32)],
    )
    return pl.pallas_call(
        functools.partial(_dsa_attend_kernel, nh=nh, rg=rg),
        grid_spec=grid_spec,
        out_shape=jax.ShapeDtypeStruct((b, s, nh * HEAD_DIM), F32),
        compiler_params=_params("arbitrary", "arbitrary"),
        name="dsa_attend",
    )(jnp.asarray(qi_tab), jnp.asarray(kb_tab), qlat, c, mask, bias, wuv2)


def _t5_bucket(rel):
    n = jnp.maximum(rel, 0)
    max_exact = REL_BUCKETS // 2
    large = max_exact + (jnp.log(jnp.maximum(n, 1).astype(F32) / max_exact)
                         / math.log(REL_MAX_DIST / max_exact) * (REL_BUCKETS - max_exact)).astype(I32)
    large = jnp.minimum(large, REL_BUCKETS - 1)
    return jnp.where(n < max_exact, n, large)


def _bias_tiles(rel_bias, tq, tk):
    assert tk >= REL_MAX_DIST
    rel0 = jnp.arange(tq, dtype=I32)[:, None] - jnp.arange(tk, dtype=I32)[None, :]
    shifted = rel_bias - rel_bias[REL_BUCKETS - 1]
    buckets = jnp.stack([_t5_bucket(rel0 + (1 - j) * tk) for j in range(tq // tk + 1)])
    onehot = jax.nn.one_hot(buckets, REL_BUCKETS, dtype=F32)
    return jnp.einsum('dijb,bh->dhij', onehot, shifted, precision=lax.Precision.HIGHEST)


def _post_kernel(x_ref, tok_ref, mq_ref, kv_ref, wout_ref, g_ref, b_ref, x1_ref, *, tok_w):
    mq = mq_ref[0]
    kv = kv_ref[0]
    y = _dot(tok_ref[0].astype(BF16), wout_ref[:tok_w, :])
    for h in range(MEM_HEADS):
        q = mq[:, h * HEAD_DIM:(h + 1) * HEAD_DIM].astype(BF16)
        k = kv[:, h * HEAD_DIM:(h + 1) * HEAD_DIM]
        v = kv[:, MEM_WIDTH + h * HEAD_DIM:MEM_WIDTH + (h + 1) * HEAD_DIM]
        lg = _dot_nt(q, k) * HEAD_DIM ** -0.5
        p = jnp.exp(lg - jnp.max(lg, axis=1, keepdims=True)).astype(BF16)
        o = _dot(p, v) / _dot(p, jnp.ones((p.shape[1], LANES), BF16))[:, :HEAD_DIM]
        y = y + _dot(o.astype(BF16), wout_ref[tok_w + h * HEAD_DIM:tok_w + (h + 1) * HEAD_DIM, :])
    x1_ref[0] = _layer_norm(DN_ALPHA * x_ref[0] + y, g_ref[...], b_ref[...])


def _post_mixer(x, tok, mq, kv, w_out, g, b, ts):
    bsz, s, d = x.shape
    tok_w = tok.shape[2]
    const = lambda shape: pl.BlockSpec(shape, lambda i, j: (0,) * len(shape))
    tile = lambda w_: pl.BlockSpec((1, ts, w_), lambda i, j: (i, j, 0))
    return pl.pallas_call(
        functools.partial(_post_kernel, tok_w=tok_w),
        grid=(bsz, s // ts),
        in_specs=[tile(d), tile(tok_w), tile(mq.shape[2]),
                  pl.BlockSpec((1,) + kv.shape[1:], lambda i, j: (i, 0, 0)),
                  const(w_out.shape), const(g.shape), const(b.shape)],
        out_specs=tile(d),
        out_shape=jax.ShapeDtypeStruct((bsz, s, d), F32),
        compiler_params=_params("arbitrary", "arbitrary"),
        name="post_mixer",
    )(x, tok, mq, kv, w_out, g, b)


def _router_kernel(x_ref, wh_ref, wl_ref, rb_ref, e_ref, gate_ref, rank_ref, cnt_ref, carry_ref):
    @pl.when(pl.program_id(0) == 0)
    def _():
        carry_ref[...] = jnp.zeros_like(carry_ref)

    x = x_ref[...]
    tr = x.shape[0]
    xh = x.astype(BF16)
    xl = (x - xh.astype(F32)).astype(BF16)
    lg = _dot(xh, wh_ref[...]) + (_dot(xl, wh_ref[...]) + _dot(xh, wl_ref[...])) + rb_ref[...]
    lane = lax.broadcasted_iota(I32, lg.shape, 1).astype(F32)
    lg = jnp.where(lane < N_EXPERTS, lg, -jnp.inf)

    idx, val = [], []
    onehot = jnp.zeros(lg.shape, F32)
    for _ in range(TOP_K):
        m = jnp.max(lg, axis=1, keepdims=True)
        i = jnp.min(jnp.where(lg == m, lane, float(LANES)), axis=1, keepdims=True)
        hit = lane == i
        idx.append(i)
        val.append(m)
        onehot = jnp.where(hit, 1.0, onehot)
        lg = jnp.where(hit, -jnp.inf, lg)

    ex = [jnp.exp(v - val[0]) for v in val]
    den = ex[0] + ex[1] + ex[2] + ex[3]

    r_i = lax.broadcasted_iota(I32, (tr, tr), 0)
    c_i = lax.broadcasted_iota(I32, (tr, tr), 1)
    tri = jnp.where(c_i < r_i, 1.0, 0.0).astype(BF16)
    before = _dot(tri, onehot.astype(BF16)) + carry_ref[...]
    carry_ref[...] = carry_ref[...] + jnp.sum(onehot, axis=0, keepdims=True)
    cnt_ref[...] = carry_ref[...]

    lane_k = lax.broadcasted_iota(I32, (tr, TOP_K), 1)
    e_out = jnp.zeros((tr, TOP_K), I32)
    g_out = jnp.zeros((tr, TOP_K), F32)
    r_out = jnp.zeros((tr, TOP_K), I32)
    for k in range(TOP_K):
        rk = jnp.sum(jnp.where(lane == idx[k], before, 0.0), axis=1, keepdims=True).astype(I32)
        e_out = jnp.where(lane_k == k, idx[k].astype(I32), e_out)
        g_out = jnp.where(lane_k == k, ex[k] / den, g_out)
        r_out = jnp.where(lane_k == k, rk, r_out)
    e_ref[...] = e_out
    gate_ref[...] = g_out
    rank_ref[...] = r_out


def _router(x, wh, wl, rb, tr):
    t, d = x.shape
    const = lambda shape: pl.BlockSpec(shape, lambda i: (0,) * len(shape))
    small = pl.BlockSpec((tr, TOP_K), lambda i: (i, 0))
    return pl.pallas_call(
        _router_kernel,
        grid=(t // tr,),
        in_specs=[pl.BlockSpec((tr, d), lambda i: (i, 0)), const(wh.shape), const(wl.shape),
                  const(rb.shape)],
        out_specs=[small, small, small, const((1, LANES))],
        out_shape=[jax.ShapeDtypeStruct((t, TOP_K), I32), jax.ShapeDtypeStruct((t, TOP_K), F32),
                   jax.ShapeDtypeStruct((t, TOP_K), I32), jax.ShapeDtypeStruct((1, LANES), F32)],
        scratch_shapes=[pltpu.VMEM((1, LANES), F32)],
        compiler_params=_params("arbitrary"),
        name="router",
    )(x, wh, wl, rb)


def _dest_kernel(e_ref, rank_ref, start_ref, dest_ref):
    e = e_ref[...]
    lane = lax.broadcasted_iota(I32, (e.shape[0], LANES), 1)
    lane_k = lax.broadcasted_iota(I32, e.shape, 1)
    out = rank_ref[...]
    for k in range(TOP_K):
        st = jnp.sum(jnp.where(lane == e[:, k:k + 1], start_ref[...], 0.0), axis=1, keepdims=True)
        out = out + jnp.where(lane_k == k, st.astype(I32), 0)
    dest_ref[...] = out


def _dest(e, rank, start, tr):
    t = e.shape[0]
    small = pl.BlockSpec((tr, TOP_K), lambda i: (i, 0))
    return pl.pallas_call(
        _dest_kernel,
        grid=(t // tr,),
        in_specs=[small, small, pl.BlockSpec((1, LANES), lambda i: (0, 0))],
        out_specs=small,
        out_shape=jax.ShapeDtypeStruct((t, TOP_K), I32),
        compiler_params=_params("arbitrary"),
        name="dest",
    )(e, rank, start)


def _dispatch_kernel(dest_ref, x_ref, xs_in_ref, xs_ref, sem):
    del xs_in_ref
    tt = x_ref.shape[0]

    def row_copy(i, k):
        d = dest_ref[i * TOP_K + k]
        return pltpu.make_async_copy(x_ref.at[pl.ds(i, 1)], xs_ref.at[pl.ds(d, 1)], sem)

    def issue(i, _):
        for k in range(TOP_K):
            row_copy(i, k).start()
        return 0

    def drain(i, _):
        for k in range(TOP_K):
            row_copy(i, k).wait()
        return 0

    lax.fori_loop(0, tt, issue, 0, unroll=8)
    lax.fori_loop(0, tt, drain, 0, unroll=8)


def _dispatch(dest_flat, x, rows, tt):
    t, d = x.shape
    xs0 = jnp.zeros((rows, d), x.dtype)
    return pl.pallas_call(
        _dispatch_kernel,
        grid=(t // tt,),
        in_specs=[pl.BlockSpec((tt * TOP_K,), lambda i: (i,), memory_space=pltpu.SMEM),
                  pl.BlockSpec((tt, d), lambda i: (i, 0)),
                  pl.BlockSpec(memory_space=pl.ANY)],
        out_specs=pl.BlockSpec(memory_space=pl.ANY),
        out_shape=jax.ShapeDtypeStruct((rows, d), x.dtype),
        scratch_shapes=[pltpu.SemaphoreType.DMA(())],
        input_output_aliases={2: 0},
        compiler_params=_params("arbitrary"),
        name="dispatch",
    )(dest_flat, x, xs0)


def _ffn_kernel(be_ref, xs_ref, w1_ref, b1_ref, w2_ref, b2_ref, ys_ref):
    del be_ref
    de = w2_ref.shape[1]
    h = _dot(xs_ref[...].astype(BF16), w1_ref[0]) + b1_ref[0]
    gt = jnp.minimum(h[:, :de], SWIGLU_LIMIT)
    up = jnp.clip(h[:, de:], -SWIGLU_LIMIT, SWIGLU_LIMIT)
    act = (up + 1.0) * (gt * _sigmoid(SWIGLU_ALPHA * gt))
    ys_ref[...] = _dot(act.astype(BF16), w2_ref[0]) + b2_ref[0]


def _expert_ffn(blk_e, xs, w1, b1, w2, b2, tm):
    rows, d = xs.shape
    grid_spec = pltpu.PrefetchScalarGridSpec(
        num_scalar_prefetch=1,
        grid=(rows // tm,),
        in_specs=[pl.BlockSpec((tm, d), lambda i, be: (i, 0)),
                  pl.BlockSpec((1,) + w1.shape[1:], lambda i, be: (be[i], 0, 0)),
                  pl.BlockSpec((1,) + b1.shape[1:], lambda i, be: (be[i], 0, 0)),
                  pl.BlockSpec((1,) + w2.shape[1:], lambda i, be: (be[i], 0, 0)),
                  pl.BlockSpec((1,) + b2.shape[1:], lambda i, be: (be[i], 0, 0))],
        out_specs=pl.BlockSpec((tm, d), lambda i, be: (i, 0)),
    )
    return pl.pallas_call(
        _ffn_kernel,
        grid_spec=grid_spec,
        out_shape=jax.ShapeDtypeStruct((rows, d), F32),
        compiler_params=_params("arbitrary"),
        name="expert_ffn",
    )(blk_e, xs, w1, b1, w2, b2)


def _combine_kernel(dest_ref, dest_next_ref, x_ref, gate_ref, g_ref, b_ref, ys_ref, out_ref,
                    buf, sem):
    tt = x_ref.shape[0]
    step = pl.program_id(0)
    slot = step % 2

    def row_copy(idx_ref, to, i, k):
        d = idx_ref[i * TOP_K + k]
        return pltpu.make_async_copy(ys_ref.at[pl.ds(d, 1)], buf.at[to, k, pl.ds(i, 1)],
                                     sem.at[to])

    def issue(idx_ref, to):
        def body(i, _):
            for k in range(TOP_K):
                row_copy(idx_ref, to, i, k).start()
            return 0
        lax.fori_loop(0, tt, body, 0, unroll=8)

    @pl.when(step == 0)
    def _():
        issue(dest_ref, slot)

    @pl.when(step + 1 < pl.num_programs(0))
    def _():
        issue(dest_next_ref, 1 - slot)

    def drain(i, _):
        for k in range(TOP_K):
            row_copy(dest_ref, slot, i, k).wait()
        return 0

    lax.fori_loop(0, tt, drain, 0, unroll=8)

    gate = gate_ref[...]
    ffn = gate[:, 0:1] * buf[slot, 0]
    for k in range(1, TOP_K):
        ffn = ffn + gate[:, k:k + 1] * buf[slot, k]
    out_ref[...] = _layer_norm(DN_ALPHA * x_ref[...] + ffn, g_ref[...], b_ref[...])


def _combine(dest_flat, x, gate, g, b, ys, tt):
    t, d = x.shape
    const = lambda shape: pl.BlockSpec(shape, lambda i: (0,) * len(shape))
    nstep = t // tt
    return pl.pallas_call(
        _combine_kernel,
        grid=(nstep,),
        in_specs=[pl.BlockSpec((tt * TOP_K,), lambda i: (i,), memory_space=pltpu.SMEM),
                  pl.BlockSpec((tt * TOP_K,), lambda i: (jnp.minimum(i + 1, nstep - 1),),
                               memory_space=pltpu.SMEM),
                  pl.BlockSpec((tt, d), lambda i: (i, 0)),
                  pl.BlockSpec((tt, TOP_K), lambda i: (i, 0)),
                  const(g.shape), const(b.shape),
                  pl.BlockSpec(memory_space=pl.ANY)],
        out_specs=pl.BlockSpec((tt, d), lambda i: (i, 0)),
        out_shape=jax.ShapeDtypeStruct((t, d), F32),
        scratch_shapes=[pltpu.VMEM((2, TOP_K, tt, d), F32), pltpu.SemaphoreType.DMA((2,))],
        compiler_params=_params("arbitrary"),
        name="combine",
    )(dest_flat, dest_flat, x, gate, g, b, ys)


def _moe_layer(x, router_w, router_b, w1, b1, w2, b2, g, b, tiles):
    t, d = x.shape
    tm, tt, tr = tiles["tm"], tiles["tt"], tiles["tr"]
    wpad = jnp.pad(router_w, ((0, 0), (0, LANES - N_EXPERTS)))
    wh = wpad.astype(BF16)
    wl = (wpad - wh.astype(F32)).astype(BF16)
    rb = jnp.pad(router_b, (0, LANES - N_EXPERTS))[None, :]
    e, gate, rank, cnt = _router(x, wh, wl, rb, tr)

    counts = cnt[0, :N_EXPERTS].astype(I32)
    padded = (counts + tm - 1) // tm * tm
    pend = jnp.cumsum(padded)
    start = jnp.pad(pend - padded, (0, LANES - N_EXPERTS))[None, :].astype(F32)
    ntile = t * TOP_K // tm + N_EXPERTS
    tile_row = jnp.arange(ntile, dtype=I32) * tm
    blk_e = jnp.minimum(jnp.sum(pend[None, :] <= tile_row[:, None], axis=1), N_EXPERTS - 1).astype(I32)

    dest = _dest(e, rank, start, tr).reshape(-1)
    xs = _dispatch(dest, x, ntile * tm, tt)
    ys = _expert_ffn(blk_e, xs, w1.astype(BF16), b1[:, None, :], w2.astype(BF16), b2[:, None, :], tm)
    return _combine(dest, x, gate, g[None, :], b[None, :], ys, tt)


def _block_diag(w):
    n, c, _ = w.shape
    eye = jnp.eye(n, dtype=w.dtype)
    return (eye[:, None, :, None] * w[:, :, None, :]).reshape(n * c, n * c)


def kernel(x, mem, rel_bias, a_w_in, a_conv_w, a_conv_b, a_wr, a_br, a_wi, a_bi, a_lambda, b_w_in, b_kv_norm_g, b_w_uk, b_w_uv, b_idx_norm_g, b_idx_norm_b, w_mem_kv, w_out, ln1_g, ln1_b, router_w, router_b, exp_w1, exp_b1, exp_w2, exp_b2, ln2_g, ln2_b):
    bsz, seq, d = x.shape
    tiles = _tiles(seq)
    ts = tiles["ts"]
    row = lambda v: v[None, :]

    def finish_layer(layer, x, tok, mq):
        kv = _matmul(mem.reshape(-1, d), w_mem_kv[layer].astype(BF16), mem.shape[1])
        kv = kv.astype(BF16).reshape(bsz, mem.shape[1], -1)
        x1 = _post_mixer(x, tok, mq, kv, w_out[layer].astype(BF16), row(ln1_g[layer]),
                         row(ln1_b[layer]), ts)
        x2 = _moe_layer(x1.reshape(-1, d), router_w[layer], router_b[layer], exp_w1[layer],
                        exp_b1[layer], exp_w2[layer], exp_b2[layer], ln2_g[layer], ln2_b[layer],
                        tiles)
        return x2.reshape(bsz, seq, d)

    tok, mq = _rglru_front(x, a_w_in[0].astype(BF16), a_conv_w[0], row(a_conv_b[0]),
                           _block_diag(a_wr[0]).astype(BF16), row(a_br[0]),
                           _block_diag(a_wi[0]).astype(BF16), row(a_bi[0]), row(a_lambda[0]), ts)
    x = finish_layer(0, x, tok, mq)

    nh = b_w_uk.shape[2]
    tok_w = nh * HEAD_DIM
    q0, c0, iq0, mq0, ik0, iw0, total = _dsa_layout(tok_w)
    w = b_w_in[0]
    o_c, o_iq = tok_w, tok_w + KV_LATENT
    o_ik = o_iq + IDX_HEADS * IDX_DIM
    o_iw = o_ik + IDX_DIM
    o_mq = o_iw + IDX_HEADS
    w_re = jnp.zeros((d, total), F32)
    w_re = w_re.at[:, q0:q0 + tok_w].set(w[:, :tok_w])
    w_re = w_re.at[:, c0:c0 + KV_LATENT].set(w[:, o_c:o_iq])
    w_re = w_re.at[:, iq0:iq0 + IDX_HEADS * IDX_DIM].set(w[:, o_iq:o_ik])
    w_re = w_re.at[:, ik0:ik0 + IDX_DIM].set(w[:, o_ik:o_iw])
    w_re = w_re.at[:, iw0:iw0 + IDX_HEADS].set(w[:, o_iw:o_mq])
    w_re = w_re.at[:, mq0:mq0 + MEM_WIDTH].set(w[:, o_mq:])
    wuk = jnp.transpose(b_w_uk[0], (1, 2, 0)).astype(BF16)
    wuv = jnp.transpose(b_w_uv[0], (1, 0, 2))
    wuv2 = jnp.zeros((nh // 2, 2 * KV_LATENT, 2 * HEAD_DIM), F32)
    wuv2 = wuv2.at[:, :KV_LATENT, :HEAD_DIM].set(wuv[0::2])
    wuv2 = wuv2.at[:, KV_LATENT:, HEAD_DIM:].set(wuv[1::2]).astype(BF16)
    qlat, c, iq, ik, iw, mq = _dsa_proj(x, w_re.astype(BF16), wuk, row(b_kv_norm_g[0]),
                                        row(b_idx_norm_g[0]), row(b_idx_norm_b[0]), ts)
    k_sel = min(TOPK_MAX, seq // 4)
    chunk = tiles["chunk"]
    ikt = jnp.swapaxes(ik.reshape(bsz, MASK_BITS, chunk, IDX_DIM), 2, 3)
    mask = _dsa_select(iq, iw, ikt, chunk, k_sel)
    tok = _dsa_attend(qlat, c, mask, _bias_tiles(rel_bias, tiles["tq_att"], chunk), wuv2,
                      tiles["tq_att"], chunk)
    x = finish_layer(1, x, tok, mq)
    return x
```

```python
import functools
import math

import jax
import jax.numpy as jnp
import numpy as np
from jax import lax
from jax.experimental import pallas as pl
from jax.experimental.pallas import tpu as pltpu

F32 = jnp.float32
BF16 = jnp.bfloat16
I32 = jnp.int32

HEAD_DIM = 64
MEM_HEADS = 4
MEM_WIDTH = MEM_HEADS * HEAD_DIM
LRU_C = 8.0
CONV_W = 4
KV_LATENT = 128
IDX_HEADS = 4
IDX_DIM = 64
TOPK_MAX = 256
REL_BUCKETS = 32
REL_MAX_DIST = 128
N_EXPERTS = 32
TOP_K = 4
SWIGLU_LIMIT = 7.0
SWIGLU_ALPHA = 1.702
DEPTH = 2
DN_ALPHA = (2 * DEPTH) ** 0.25
LN_EPS = 1e-5
RMS_EPS = 1e-6

SUBLANES = 8
LANES = 128
MASK_BITS = 32
INT_MIN = -2 ** 31
NEG_BIG = -1e30
VMEM_LIMIT = 56 * 1024 * 1024


def _tiles(seq):
    chunk = seq // MASK_BITS
    return dict(
        ts=min(512, seq),
        chunk=chunk,
        tq_att=2 * chunk,
        tt=256,
        tm=512,
        tr=512,
    )


def _dot(a, b):
    return jnp.dot(a, b, preferred_element_type=F32)


def _dot_nt(a, b):
    return lax.dot_general(a, b, (((1,), (1,)), ((), ())), preferred_element_type=F32)


def _sigmoid(x):
    return 1.0 / (1.0 + jnp.exp(-x))


def _layer_norm(z, g, b):
    mu = jnp.mean(z, axis=-1, keepdims=True)
    zc = z - mu
    var = jnp.mean(zc * zc, axis=-1, keepdims=True)
    return zc * lax.rsqrt(var + LN_EPS) * g + b


def _params(*sem):
    return pltpu.CompilerParams(dimension_semantics=sem, vmem_limit_bytes=VMEM_LIMIT)


def _mm_kernel(a_ref, b_ref, o_ref):
    o_ref[...] = _dot(a_ref[...].astype(BF16), b_ref[...])


def _matmul(a, b, tm):
    m, k = a.shape
    n = b.shape[1]
    return pl.pallas_call(
        _mm_kernel,
        grid=(m // tm,),
        in_specs=[pl.BlockSpec((tm, k), lambda i: (i, 0)),
                  pl.BlockSpec((k, n), lambda i: (0, 0))],
        out_specs=pl.BlockSpec((tm, n), lambda i: (i, 0)),
        out_shape=jax.ShapeDtypeStruct((m, n), F32),
        compiler_params=_params("arbitrary"),
        name="matmul",
    )(a, b)


def _rglru_kernel(x_ref, win_ref, cw_ref, cb_ref, wr_ref, br_ref, wi_ref, bi_ref, lam_ref,
                  tok_ref, mq_ref, tail_ref, h_ref, a_s, u_s, *, tok_w):
    s = pl.program_id(1)

    @pl.when(s == 0)
    def _():
        tail_ref[...] = jnp.zeros_like(tail_ref)
        h_ref[...] = jnp.zeros_like(h_ref)

    proj = _dot(x_ref[0].astype(BF16), win_ref[...])
    xb = proj[:, :tok_w]
    gb = proj[:, tok_w:2 * tok_w]
    mq_ref[0] = proj[:, 2 * tok_w:]
    ts = xb.shape[0]

    xe = jnp.concatenate([tail_ref[...], xb], axis=0)
    tail_ref[...] = xb[ts - SUBLANES:, :]
    xc = xb * cw_ref[CONV_W - 1:CONV_W, :] + cb_ref[...]
    for d in range(1, CONV_W):
        sh = pltpu.roll(xe, d, 0)[SUBLANES:, :]
        xc = xc + sh * cw_ref[CONV_W - 1 - d:CONV_W - d, :]

    xcb = xc.astype(BF16)
    r = _sigmoid(_dot(xcb, wr_ref[...]) + br_ref[...])
    ig = _sigmoid(_dot(xcb, wi_ref[...]) + bi_ref[...])
    nl = -lam_ref[...]
    softplus = jnp.maximum(nl, 0.0) + jnp.log1p(jnp.exp(-jnp.abs(nl)))
    log_a = (-LRU_C) * r * softplus
    a = jnp.exp(log_a)
    th = jnp.tanh(log_a)
    u = jnp.sqrt(-2.0 * th / (1.0 - th)) * (ig * xc)

    row = lax.broadcasted_iota(I32, a.shape, 0) & (SUBLANES - 1)
    for d in (1, 2, 4):
        m = row >= d
        u = jnp.where(m, a * pltpu.roll(u, d, 0) + u, u)
        a = jnp.where(m, a * pltpu.roll(a, d, 0), a)
    a_s[...] = a
    u_s[...] = u

    def group(g, h):
        sl = pl.ds(pl.multiple_of(g * SUBLANES, SUBLANES), SUBLANES)
        hg = u_s[sl, :] + a_s[sl, :] * h
        u_s[sl, :] = hg
        return jnp.broadcast_to(hg[SUBLANES - 1:, :], hg.shape)

    h_ref[...] = lax.fori_loop(0, ts // SUBLANES, group, h_ref[...])

    gelu = 0.5 * gb * (1.0 + jnp.tanh(math.sqrt(2.0 / math.pi) * (gb + 0.044715 * (gb * gb * gb))))
    tok_ref[0] = u_s[...] * gelu


def _rglru_front(x, w_in, conv_w, conv_b, wr, br, wi, bi, lam, ts):
    b, s, d = x.shape
    tok_w = conv_w.shape[-1]
    mem_w = w_in.shape[-1] - 2 * tok_w
    const = lambda shape: pl.BlockSpec(shape, lambda i, j: (0,) * len(shape))
    return pl.pallas_call(
        functools.partial(_rglru_kernel, tok_w=tok_w),
        grid=(b, s // ts),
        in_specs=[pl.BlockSpec((1, ts, d), lambda i, j: (i, j, 0)),
                  const(w_in.shape), const(conv_w.shape), const(conv_b.shape),
                  const(wr.shape), const(br.shape), const(wi.shape), const(bi.shape),
                  const(lam.shape)],
        out_specs=[pl.BlockSpec((1, ts, tok_w), lambda i, j: (i, j, 0)),
                   pl.BlockSpec((1, ts, mem_w), lambda i, j: (i, j, 0))],
        out_shape=[jax.ShapeDtypeStruct((b, s, tok_w), F32),
                   jax.ShapeDtypeStruct((b, s, mem_w), F32)],
        scratch_shapes=[pltpu.VMEM((SUBLANES, tok_w), F32), pltpu.VMEM((SUBLANES, tok_w), F32),
                        pltpu.VMEM((ts, tok_w), F32), pltpu.VMEM((ts, tok_w), F32)],
        compiler_params=_params("arbitrary", "arbitrary"),
        name="rglru_front",
    )(x, w_in, conv_w, conv_b, wr, br, wi, bi, lam)


def _dsa_layout(tok_w):
    q0 = 0
    c0 = q0 + tok_w
    iq0 = c0 + KV_LATENT
    mq0 = iq0 + IDX_HEADS * IDX_DIM
    ik0 = mq0 + MEM_WIDTH
    iw0 = ik0 + LANES
    total = iw0 + LANES
    return q0, c0, iq0, mq0, ik0, iw0, total


def _dsa_proj_kernel(x_ref, w_ref, wuk_ref, kvg_ref, ig_ref, ib_ref,
                     qlat_ref, c_ref, iq_ref, ik_ref, iw_ref, mq_ref, *, tok_w):
    q0, c0, iq0, mq0, ik0, iw0, _ = _dsa_layout(tok_w)
    proj = _dot(x_ref[0].astype(BF16), w_ref[...])
    for h in range(tok_w // HEAD_DIM):
        qh = proj[:, q0 + h * HEAD_DIM:q0 + (h + 1) * HEAD_DIM].astype(BF16)
        ql = _dot(qh, wuk_ref[h]) * HEAD_DIM ** -0.5
        qlat_ref[0, h] = ql.astype(BF16)
    c = proj[:, c0:c0 + KV_LATENT]
    c = c * lax.rsqrt(jnp.mean(c * c, axis=-1, keepdims=True) + RMS_EPS) * kvg_ref[...]
    c_ref[0] = c.astype(BF16)
    iq_ref[0] = proj[:, iq0:iq0 + IDX_HEADS * IDX_DIM].astype(BF16)
    ik = _layer_norm(proj[:, ik0:ik0 + IDX_DIM], ig_ref[...], ib_ref[...])
    ik_ref[0] = ik.astype(BF16)
    iw_ref[0] = proj[:, iw0:iw0 + IDX_HEADS] * (IDX_HEADS ** -0.5 * IDX_DIM ** -0.5)
    mq_ref[0] = proj[:, mq0:mq0 + MEM_WIDTH]


def _dsa_proj(x, w, wuk, kvg, ig, ib, ts):
    b, s, d = x.shape
    nh = wuk.shape[0]
    tok_w = nh * HEAD_DIM
    const = lambda shape: pl.BlockSpec(shape, lambda i, j: (0,) * len(shape))
    tile = lambda w_: pl.BlockSpec((1, ts, w_), lambda i, j: (i, j, 0))
    widths = [KV_LATENT, IDX_HEADS * IDX_DIM, IDX_DIM, IDX_HEADS, MEM_WIDTH]
    dtypes = [BF16, BF16, BF16, F32, F32]
    return pl.pallas_call(
        functools.partial(_dsa_proj_kernel, tok_w=tok_w),
        grid=(b, s // ts),
        in_specs=[tile(d), const(w.shape), const(wuk.shape), const(kvg.shape),
                  const(ig.shape), const(ib.shape)],
        out_specs=[pl.BlockSpec((1, nh, ts, KV_LATENT), lambda i, j: (i, 0, j, 0))]
        + [tile(w_) for w_ in widths],
        out_shape=[jax.ShapeDtypeStruct((b, nh, s, KV_LATENT), BF16)]
        + [jax.ShapeDtypeStruct((b, s, w_), t) for w_, t in zip(widths, dtypes)],
        compiler_params=_params("arbitrary", "arbitrary"),
        name="dsa_proj",
    )(x, w, wuk, kvg, ig, ib)


def _row_popcount(words):
    pc = lax.population_count(words)
    part = pc[:, :LANES]
    for i in range(1, words.shape[1] // LANES):
        part = part + pc[:, i * LANES:(i + 1) * LANES]
    return jnp.sum(part.astype(F32), axis=1, keepdims=True)


def _dsa_select_kernel(iq_ref, iw_ref, ikt_ref, mask_ref, keys_ref, iwb_ref, tie_ref,
                       *, k_sel, chunk):
    qi = pl.program_id(1)
    tq = iq_ref.shape[1]
    iq = iq_ref[0]
    iw = iw_ref[0]
    for h in range(IDX_HEADS):
        iwb_ref[h] = jnp.broadcast_to(iw[:, h:h + 1], (tq, chunk))
    iqh = [iq[:, h * IDX_DIM:(h + 1) * IDX_DIM] for h in range(IDX_HEADS)]

    def ukey(c):
        ikc = ikt_ref[0, c]
        sc = jnp.maximum(_dot(iqh[0], ikc), 0.0) * iwb_ref[0]
        for h in range(1, IDX_HEADS):
            sc = sc + jnp.maximum(_dot(iqh[h], ikc), 0.0) * iwb_ref[h]
        bits = lax.bitcast_convert_type(sc, I32)
        word = bits ^ ((bits >> 31) | INT_MIN)
        return word + (word == 0x7FFFFFFF).astype(I32)

    def score(c, _):
        keys_ref[c] = ukey(c)
        return 0

    lax.fori_loop(0, qi, score, 0)
    row = lax.broadcasted_iota(I32, (tq, chunk), 0)
    lane = lax.broadcasted_iota(I32, (tq, chunk), 1)
    keys_ref[qi] = jnp.where(lane <= row, ukey(qi), 0)

    def clear(c, _):
        keys_ref[c] = jnp.zeros((tq, chunk), I32)
        return 0

    lax.fori_loop(qi + 1, MASK_BITS, clear, 0)

    def transpose(g, _):
        rows = pl.ds(pl.multiple_of(g * SUBLANES, SUBLANES), SUBLANES)
        for half in range(chunk // LANES):
            cols = slice(half * LANES, (half + 1) * LANES)
            a = [keys_ref[c, rows, cols] for c in range(MASK_BITS)]
            j, m = MASK_BITS // 2, 0x0000FFFF
            while j:
                k = 0
                while k < MASK_BITS:
                    t = (lax.shift_right_logical(a[k], j) ^ a[k + j]) & m
                    a[k] = a[k] ^ (t << j)
                    a[k + j] = a[k + j] ^ t
                    k = (k + j + 1) & ~j
                j >>= 1
                m = (m ^ (m << j)) & 0xFFFFFFFF
                m = m - (1 << 32) if m >= (1 << 31) else m
            for c in range(MASK_BITS):
                keys_ref[c, rows, cols] = a[c]
        return 0

    lax.fori_loop(0, tq // SUBLANES, transpose, 0)

    def plane_step(i, carry):
        active, greater, need, any_one = carry
        ones = active & keys_ref[MASK_BITS - 1 - i]
        cnt = _row_popcount(ones)
        take = cnt >= need
        active = jnp.where(take, ones, active ^ ones)
        greater = jnp.where(take, greater, greater | ones)
        need = jnp.where(take, need, need - cnt)
        return active, greater, need, jnp.where(take, 1.0, any_one)

    full = jnp.full((tq, chunk), -1, I32)
    ties, greater, need, any_one = lax.fori_loop(
        0, MASK_BITS, plane_step,
        (full, jnp.zeros((tq, chunk), I32), jnp.full((tq, 1), float(k_sel), F32),
         jnp.zeros((tq, 1), F32)))
    ties = jnp.where(any_one > 0.0, ties, 0)
    tie_ref[...] = ties

    @pl.when(jnp.max(_row_popcount(ties) - need) > 0.0)
    def _():
        shift = chunk.bit_length() - 1
        lane_i = lax.broadcasted_iota(I32, (tq, chunk), 1)

        def below(v, inclusive):
            vh = jnp.broadcast_to(v >> shift, (tq, chunk))
            vl = jnp.broadcast_to(v & (chunk - 1), (tq, chunk))
            top = jnp.int32(1) << vh
            edge = (lane_i <= vl) if inclusive else (lane_i < vl)
            return (top - 1) | jnp.where(edge, top, 0)

        nbits = (MASK_BITS * chunk).bit_length() - 1

        def idx_step(i, j):
            cand = j | (jnp.int32(1) << (nbits - 1 - i))
            cnt = _row_popcount(ties & below(cand, False))
            return jnp.where(cnt < need, cand, j)

        last = lax.fori_loop(0, nbits, idx_step, jnp.zeros((tq, 1), I32))
        tie_ref[...] = ties & below(last, True)

    mask_ref[0] = greater | tie_ref[...]


def _dsa_select(iq, iw, ikt, chunk, k_sel):
    b, s, _ = iq.shape
    tq = chunk
    return pl.pallas_call(
        functools.partial(_dsa_select_kernel, k_sel=k_sel, chunk=chunk),
        grid=(b, s // tq),
        in_specs=[pl.BlockSpec((1, tq, iq.shape[2]), lambda i, j: (i, j, 0)),
                  pl.BlockSpec((1, tq, iw.shape[2]), lambda i, j: (i, j, 0)),
                  pl.BlockSpec((1,) + ikt.shape[1:], lambda i, j: (i, 0, 0, 0))],
        out_specs=pl.BlockSpec((1, tq, chunk), lambda i, j: (i, j, 0)),
        out_shape=jax.ShapeDtypeStruct((b, s, chunk), I32),
        scratch_shapes=[pltpu.VMEM((MASK_BITS, tq, chunk), I32),
                        pltpu.VMEM((IDX_HEADS, tq, chunk), F32),
                        pltpu.VMEM((tq, chunk), I32)],
        compiler_params=_params("arbitrary", "arbitrary"),
        name="dsa_select",
    )(iq, iw, ikt)


def _dsa_attend_kernel(qi_ref, kb_ref, qlat_ref, c_ref, mask_ref, bias_ref, wuv_ref, out_ref,
                       m_ref, acc_ref, madd_ref, *, nh, rg):
    p_id = pl.program_id(1)
    qi = qi_ref[p_id]
    kb = kb_ref[p_id]
    tq, tk = madd_ref.shape
    ngroups = tq // rg
    qmul = tq // tk
    first_near = qmul * qi - 1

    @pl.when(kb == 0)
    def _():
        m_ref[...] = jnp.full(m_ref.shape, NEG_BIG, F32)
        acc_ref[...] = jnp.zeros_like(acc_ref)

    madd_ref[...] = jnp.where(((mask_ref[0] >> kb) & 1) == 1, 0.0, NEG_BIG)
    c = c_ref[0]
    c_one = jnp.concatenate([c, jnp.ones_like(c)], axis=1)

    def heads(with_bias):
        for i in range(nh * ngroups):
            h, g = divmod(i, ngroups)
            rows = pl.ds(i * rg, rg)
            qrows = pl.ds(g * rg, rg)
            lg = _dot_nt(qlat_ref[0, h, qrows, :], c) + madd_ref[qrows, :]
            if with_bias:
                lg = lg + bias_ref[0, h, qrows, :]
            m_old = m_ref[rows, :]
            m_new = jnp.maximum(m_old, jnp.max(lg, axis=1, keepdims=True))
            alpha = jnp.exp(m_old - m_new)
            p = jnp.exp(lg - jnp.concatenate([m_new] * (tk // LANES), axis=1))
            m_ref[rows, :] = m_new
            acc_ref[rows, :] = (jnp.concatenate([alpha, alpha], axis=1) * acc_ref[rows, :]
                                + _dot(p.astype(BF16), c_one))

    @pl.when(kb >= first_near)
    def _():
        heads(True)

    @pl.when(kb < first_near)
    def _():
        heads(False)

    @pl.when(kb == first_near + qmul)
    def _():
        for j in range(nh // 2):
            r0, r1 = pl.ds(2 * j * tq, tq), pl.ds((2 * j + 1) * tq, tq)
            o = jnp.concatenate([acc_ref[r0, :KV_LATENT] / acc_ref[r0, KV_LATENT:],
                                 acc_ref[r1, :KV_LATENT] / acc_ref[r1, KV_LATENT:]], axis=1)
            out_ref[0, :, j * LANES:(j + 1) * LANES] = _dot(o.astype(BF16), wuv_ref[j])


def _dsa_attend(qlat, c, mask, bias, wuv2, tq, tk):
    b, nh, s, _ = qlat.shape
    nq = s // tq
    qmul = tq // tk
    rg = min(128, tq)
    qi_tab = np.concatenate([np.full(qmul * (q + 1), q, np.int32) for q in range(nq)])
    kb_tab = np.concatenate([np.arange(qmul * (q + 1), dtype=np.int32) for q in range(nq)])
    grid_spec = pltpu.PrefetchScalarGridSpec(
        num_scalar_prefetch=2,
        grid=(b, len(qi_tab)),
        in_specs=[pl.BlockSpec((1, nh, tq, KV_LATENT), lambda i, p, qt, kt: (i, 0, qt[p], 0)),
                  pl.BlockSpec((1, tk, c.shape[2]), lambda i, p, qt, kt: (i, kt[p], 0)),
                  pl.BlockSpec((1, tq, mask.shape[2]), lambda i, p, qt, kt: (i, qt[p], 0)),
                  pl.BlockSpec((1, nh, tq, tk),
                               lambda i, p, qt, kt: (jnp.clip(kt[p] - (qmul * qt[p] - 1), 0, qmul),
                                                     0, 0, 0)),
                  pl.BlockSpec(wuv2.shape, lambda i, p, qt, kt: (0, 0, 0))],
        out_specs=pl.BlockSpec((1, tq, nh * HEAD_DIM), lambda i, p, qt, kt: (i, qt[p], 0)),
        scratch_shapes=[pltpu.VMEM((nh * tq, LANES), F32),
                        pltpu.VMEM((nh * tq, 2 * KV_LATENT), F32),
                        pltpu.VMEM((tq, tk), F32)],
    )
    return pl.pallas_call(
        functools.partial(_dsa_attend_kernel, nh=nh, rg=rg),
        grid_spec=grid_spec,
        out_shape=jax.ShapeDtypeStruct((b, s, nh * HEAD_DIM), F32),
        compiler_params=_params("arbitrary", "arbitrary"),
        name="dsa_attend",
    )(jnp.asarray(qi_tab), jnp.asarray(kb_tab), qlat, c, mask, bias, wuv2)


def _t5_bucket(rel):
    n = jnp.maximum(rel, 0)
    max_exact = REL_BUCKETS // 2
    large = max_exact + (jnp.log(jnp.maximum(n, 1).astype(F32) / max_exact)
                         / math.log(REL_MAX_DIST / max_exact) * (REL_BUCKETS - max_exact)).astype(I32)
    large = jnp.minimum(large, REL_BUCKETS - 1)
    return jnp.where(n < max_exact, n, large)


def _bias_tiles(rel_bias, tq, tk):
    assert tk >= REL_MAX_DIST
    rel0 = jnp.arange(tq, dtype=I32)[:, None] - jnp.arange(tk, dtype=I32)[None, :]
    shifted = rel_bias - rel_bias[REL_BUCKETS - 1]
    buckets = jnp.stack([_t5_bucket(rel0 + (1 - j) * tk) for j in range(tq // tk + 1)])
    onehot = jax.nn.one_hot(buckets, REL_BUCKETS, dtype=F32)
    return jnp.einsum('dijb,bh->dhij', onehot, shifted, precision=lax.Precision.HIGHEST)


def _post_kernel(x_ref, tok_ref, mq_ref, kv_ref, wout_ref, g_ref, b_ref, x1_ref, *, tok_w):
    mq = mq_ref[0]
    kv = kv_ref[0]
    y = _dot(tok_ref[0].astype(BF16), wout_ref[:tok_w, :])
    for h in range(MEM_HEADS):
        q = mq[:, h * HEAD_DIM:(h + 1) * HEAD_DIM].astype(BF16)
        k = kv[:, h * HEAD_DIM:(h + 1) * HEAD_DIM]
        v = kv[:, MEM_WIDTH + h * HEAD_DIM:MEM_WIDTH + (h + 1) * HEAD_DIM]
        lg = _dot_nt(q, k) * HEAD_DIM ** -0.5
        p = jnp.exp(lg - jnp.max(lg, axis=1, keepdims=True)).astype(BF16)
        o = _dot(p, v) / _dot(p, jnp.ones((p.shape[1], LANES), BF16))[:, :HEAD_DIM]
        y = y + _dot(o.astype(BF16), wout_ref[tok_w + h * HEAD_DIM:tok_w + (h + 1) * HEAD_DIM, :])
    x1_ref[0] = _layer_norm(DN_ALPHA * x_ref[0] + y, g_ref[...], b_ref[...])


def _post_mixer(x, tok, mq, kv, w_out, g, b, ts):
    bsz, s, d = x.shape
    tok_w = tok.shape[2]
    const = lambda shape: pl.BlockSpec(shape, lambda i, j: (0,) * len(shape))
    tile = lambda w_: pl.BlockSpec((1, ts, w_), lambda i, j: (i, j, 0))
    return pl.pallas_call(
        functools.partial(_post_kernel, tok_w=tok_w),
        grid=(bsz, s // ts),
        in_specs=[tile(d), tile(tok_w), tile(mq.shape[2]),
                  pl.BlockSpec((1,) + kv.shape[1:], lambda i, j: (i, 0, 0)),
                  const(w_out.shape), const(g.shape), const(b.shape)],
        out_specs=tile(d),
        out_shape=jax.ShapeDtypeStruct((bsz, s, d), F32),
        compiler_params=_params("arbitrary", "arbitrary"),
        name="post_mixer",
    )(x, tok, mq, kv, w_out, g, b)


def _router_kernel(x_ref, wh_ref, wl_ref, rb_ref, e_ref, gate_ref, rank_ref, cnt_ref, carry_ref):
    @pl.when(pl.program_id(0) == 0)
    def _():
        carry_ref[...] = jnp.zeros_like(carry_ref)

    x = x_ref[...]
    tr = x.shape[0]
    xh = x.astype(BF16)
    xl = (x - xh.astype(F32)).astype(BF16)
    lg = _dot(xh, wh_ref[...]) + (_dot(xl, wh_ref[...]) + _dot(xh, wl_ref[...])) + rb_ref[...]
    lane = lax.broadcasted_iota(I32, lg.shape, 1).astype(F32)
    lg = jnp.where(lane < N_EXPERTS, lg, -jnp.inf)

    idx, val = [], []
    onehot = jnp.zeros(lg.shape, F32)
    for _ in range(TOP_K):
        m = jnp.max(lg, axis=1, keepdims=True)
        i = jnp.min(jnp.where(lg == m, lane, float(LANES)), axis=1, keepdims=True)
        hit = lane == i
        idx.append(i)
        val.append(m)
        onehot = jnp.where(hit, 1.0, onehot)
        lg = jnp.where(hit, -jnp.inf, lg)

    ex = [jnp.exp(v - val[0]) for v in val]
    den = ex[0] + ex[1] + ex[2] + ex[3]

    r_i = lax.broadcasted_iota(I32, (tr, tr), 0)
    c_i = lax.broadcasted_iota(I32, (tr, tr), 1)
    tri = jnp.where(c_i < r_i, 1.0, 0.0).astype(BF16)
    before = _dot(tri, onehot.astype(BF16)) + carry_ref[...]
    carry_ref[...] = carry_ref[...] + jnp.sum(onehot, axis=0, keepdims=True)
    cnt_ref[...] = carry_ref[...]

    lane_k = lax.broadcasted_iota(I32, (tr, TOP_K), 1)
    e_out = jnp.zeros((tr, TOP_K), I32)
    g_out = jnp.zeros((tr, TOP_K), F32)
    r_out = jnp.zeros((tr, TOP_K), I32)
    for k in range(TOP_K):
        rk = jnp.sum(jnp.where(lane == idx[k], before, 0.0), axis=1, keepdims=True).astype(I32)
        e_out = jnp.where(lane_k == k, idx[k].astype(I32), e_out)
        g_out = jnp.where(lane_k == k, ex[k] / den, g_out)
        r_out = jnp.where(lane_k == k, rk, r_out)
    e_ref[...] = e_out
    gate_ref[...] = g_out
    rank_ref[...] = r_out


def _router(x, wh, wl, rb, tr):
    t, d = x.shape
    const = lambda shape: pl.BlockSpec(shape, lambda i: (0,) * len(shape))
    small = pl.BlockSpec((tr, TOP_K), lambda i: (i, 0))
    return pl.pallas_call(
        _router_kernel,
        grid=(t // tr,),
        in_specs=[pl.BlockSpec((tr, d), lambda i: (i, 0)), const(wh.shape), const(wl.shape),
                  const(rb.shape)],
        out_specs=[small, small, small, const((1, LANES))],
        out_shape=[jax.ShapeDtypeStruct((t, TOP_K), I32), jax.ShapeDtypeStruct((t, TOP_K), F32),
                   jax.ShapeDtypeStruct((t, TOP_K), I32), jax.ShapeDtypeStruct((1, LANES), F32)],
        scratch_shapes=[pltpu.VMEM((1, LANES), F32)],
        compiler_params=_params("arbitrary"),
        name="router",
    )(x, wh, wl, rb)


def _dest_kernel(e_ref, rank_ref, start_ref, dest_ref):
    e = e_ref[...]
    lane = lax.broadcasted_iota(I32, (e.shape[0], LANES), 1)
    lane_k = lax.broadcasted_iota(I32, e.shape, 1)
    out = rank_ref[...]
    for k in range(TOP_K):
        st = jnp.sum(jnp.where(lane == e[:, k:k + 1], start_ref[...], 0.0), axis=1, keepdims=True)
        out = out + jnp.where(lane_k == k, st.astype(I32), 0)
    dest_ref[...] = out


def _dest(e, rank, start, tr):
    t = e.shape[0]
    small = pl.BlockSpec((tr, TOP_K), lambda i: (i, 0))
    return pl.pallas_call(
        _dest_kernel,
        grid=(t // tr,),
        in_specs=[small, small, pl.BlockSpec((1, LANES), lambda i: (0, 0))],
        out_specs=small,
        out_shape=jax.ShapeDtypeStruct((t, TOP_K), I32),
        compiler_params=_params("arbitrary"),
        name="dest",
    )(e, rank, start)


def _dispatch_kernel(dest_ref, meta_ref, x_ref, xs_ref, zero_ref, sem, *, tm):
    tt = x_ref.shape[0]
    ntail = (xs_ref.shape[0] - meta_ref[2 * N_EXPERTS]) // tm

    def pad_copy(e, r):
        row = meta_ref[N_EXPERTS + e] + r
        return pltpu.make_async_copy(zero_ref.at[pl.ds(0, 1)], xs_ref.at[pl.ds(row, 1)], sem)

    def tail_copy(j):
        row = pl.multiple_of(meta_ref[2 * N_EXPERTS] + j * tm, tm)
        return pltpu.make_async_copy(zero_ref, xs_ref.at[pl.ds(row, tm)], sem)

    def for_each_pad(fn):
        def per_expert(e, _):
            width = meta_ref[N_EXPERTS + e + 1] - meta_ref[N_EXPERTS + e]
            lax.fori_loop(meta_ref[e], width, lambda r, c: (fn(pad_copy(e, r)), c)[1], 0)
            return 0
        lax.fori_loop(0, N_EXPERTS, per_expert, 0)
        lax.fori_loop(0, ntail, lambda j, c: (fn(tail_copy(j)), c)[1], 0)

    @pl.when(pl.program_id(0) == 0)
    def _():
        zero_ref[...] = jnp.zeros_like(zero_ref)
        for_each_pad(lambda cp: cp.start())
        for_each_pad(lambda cp: cp.wait())

    def row_copy(i, k):
        d = dest_ref[i * TOP_K + k]
        return pltpu.make_async_copy(x_ref.at[pl.ds(i, 1)], xs_ref.at[pl.ds(d, 1)], sem)

    def issue(i, _):
        for k in range(TOP_K):
            row_copy(i, k).start()
        return 0

    def drain(i, _):
        for k in range(TOP_K):
            row_copy(i, k).wait()
        return 0

    lax.fori_loop(0, tt, issue, 0, unroll=8)
    lax.fori_loop(0, tt, drain, 0, unroll=8)


def _dispatch(dest_flat, meta, x, rows, tt, tm):
    t, d = x.shape
    return pl.pallas_call(
        functools.partial(_dispatch_kernel, tm=tm),
        grid=(t // tt,),
        in_specs=[pl.BlockSpec((tt * TOP_K,), lambda i: (i,), memory_space=pltpu.SMEM),
                  pl.BlockSpec(memory_space=pltpu.SMEM),
                  pl.BlockSpec((tt, d), lambda i: (i, 0))],
        out_specs=pl.BlockSpec(memory_space=pl.ANY),
        out_shape=jax.ShapeDtypeStruct((rows, d), x.dtype),
        scratch_shapes=[pltpu.VMEM((tm, d), x.dtype), pltpu.SemaphoreType.DMA(())],
        compiler_params=_params("arbitrary"),
        name="dispatch",
    )(dest_flat, meta, x)


def _ffn_kernel(be_ref, xs_ref, w1_ref, b1_ref, w2_ref, b2_ref, ys_ref):
    del be_ref
    de = w2_ref.shape[1]
    h = _dot(xs_ref[...].astype(BF16), w1_ref[0]) + b1_ref[0]
    gt = jnp.minimum(h[:, :de], SWIGLU_LIMIT)
    up = jnp.clip(h[:, de:], -SWIGLU_LIMIT, SWIGLU_LIMIT)
    act = (up + 1.0) * (gt * _sigmoid(SWIGLU_ALPHA * gt))
    ys_ref[...] = _dot(act.astype(BF16), w2_ref[0]) + b2_ref[0]


def _expert_ffn(blk_e, xs, w1, b1, w2, b2, tm):
    rows, d = xs.shape
    grid_spec = pltpu.PrefetchScalarGridSpec(
        num_scalar_prefetch=1,
        grid=(rows // tm,),
        in_specs=[pl.BlockSpec((tm, d), lambda i, be: (i, 0)),
                  pl.BlockSpec((1,) + w1.shape[1:], lambda i, be: (be[i], 0, 0)),
                  pl.BlockSpec((1,) + b1.shape[1:], lambda i, be: (be[i], 0, 0)),
                  pl.BlockSpec((1,) + w2.shape[1:], lambda i, be: (be[i], 0, 0)),
                  pl.BlockSpec((1,) + b2.shape[1:], lambda i, be: (be[i], 0, 0))],
        out_specs=pl.BlockSpec((tm, d), lambda i, be: (i, 0)),
    )
    return pl.pallas_call(
        _ffn_kernel,
        grid_spec=grid_spec,
        out_shape=jax.ShapeDtypeStruct((rows, d), F32),
        compiler_params=_params("arbitrary"),
        name="expert_ffn",
    )(blk_e, xs, w1, b1, w2, b2)


def _combine_kernel(dest_ref, dest_next_ref, x_ref, gate_ref, g_ref, b_ref, ys_ref, out_ref,
                    buf, sem):
    tt = x_ref.shape[0]
    step = pl.program_id(0)
    slot = step % 2

    def row_copy(idx_ref, to, i, k):
        d = idx_ref[i * TOP_K + k]
        return pltpu.make_async_copy(ys_ref.at[pl.ds(d, 1)], buf.at[to, k, pl.ds(i, 1)],
                                     sem.at[to])

    def issue(idx_ref, to):
        def body(i, _):
            for k in range(TOP_K):
                row_copy(idx_ref, to, i, k).start()
            return 0
        lax.fori_loop(0, tt, body, 0, unroll=8)

    @pl.when(step == 0)
    def _():
        issue(dest_ref, slot)

    @pl.when(step + 1 < pl.num_programs(0))
    def _():
        issue(dest_next_ref, 1 - slot)

    def drain(i, _):
        for k in range(TOP_K):
            row_copy(dest_ref, slot, i, k).wait()
        return 0

    lax.fori_loop(0, tt, drain, 0, unroll=8)

    gate = gate_ref[...]
    ffn = gate[:, 0:1] * buf[slot, 0]
    for k in range(1, TOP_K):
        ffn = ffn + gate[:, k:k + 1] * buf[slot, k]
    out_ref[...] = _layer_norm(DN_ALPHA * x_ref[...] + ffn, g_ref[...], b_ref[...])


def _combine(dest_flat, x, gate, g, b, ys, tt):
    t, d = x.shape
    const = lambda shape: pl.BlockSpec(shape, lambda i: (0,) * len(shape))
    nstep = t // tt
    return pl.pallas_call(
        _combine_kernel,
        grid=(nstep,),
        in_specs=[pl.BlockSpec((tt * TOP_K,), lambda i: (i,), memory_space=pltpu.SMEM),
                  pl.BlockSpec((tt * TOP_K,), lambda i: (jnp.minimum(i + 1, nstep - 1),),
                               memory_space=pltpu.SMEM),
                  pl.BlockSpec((tt, d), lambda i: (i, 0)),
                  pl.BlockSpec((tt, TOP_K), lambda i: (i, 0)),
                  const(g.shape), const(b.shape),
                  pl.BlockSpec(memory_space=pl.ANY)],
        out_specs=pl.BlockSpec((tt, d), lambda i: (i, 0)),
        out_shape=jax.ShapeDtypeStruct((t, d), F32),
        scratch_shapes=[pltpu.VMEM((2, TOP_K, tt, d), F32), pltpu.SemaphoreType.DMA((2,))],
        compiler_params=_params("arbitrary"),
        name="combine",
    )(dest_flat, dest_flat, x, gate, g, b, ys)


def _moe_layer(x, router_w, router_b, w1, b1, w2, b2, g, b, tiles):
    t, d = x.shape
    tm, tt, tr = tiles["tm"], tiles["tt"], tiles["tr"]
    wpad = jnp.pad(router_w, ((0, 0), (0, LANES - N_EXPERTS)))
    wh = wpad.astype(BF16)
    wl = (wpad - wh.astype(F32)).astype(BF16)
    rb = jnp.pad(router_b, (0, LANES - N_EXPERTS))[None, :]
    e, gate, rank, cnt = _router(x, wh, wl, rb, tr)

    counts = cnt[0, :N_EXPERTS].astype(I32)
    padded = (counts + tm - 1) // tm * tm
    pend = jnp.cumsum(padded)
    start = jnp.pad(pend - padded, (0, LANES - N_EXPERTS))[None, :].astype(F32)
    ntile = t * TOP_K // tm + N_EXPERTS
    tile_row = jnp.arange(ntile, dtype=I32) * tm
    blk_e = jnp.minimum(jnp.sum(pend[None, :] <= tile_row[:, None], axis=1), N_EXPERTS - 1).astype(I32)

    dest = _dest(e, rank, start, tr).reshape(-1)
    meta = jnp.concatenate([counts, pend - padded, pend[-1:]]).astype(I32)
    xs = _dispatch(dest, meta, x, ntile * tm, tt, tm)
    ys = _expert_ffn(blk_e, xs, w1.astype(BF16), b1[:, None, :], w2.astype(BF16), b2[:, None, :], tm)
    return _combine(dest, x, gate, g[None, :], b[None, :], ys, tt)


def _block_diag(w):
    n, c, _ = w.shape
    eye = jnp.eye(n, dtype=w.dtype)
    return (eye[:, None, :, None] * w[:, :, None, :]).reshape(n * c, n * c)


def kernel(x, mem, rel_bias, a_w_in, a_conv_w, a_conv_b, a_wr, a_br, a_wi, a_bi, a_lambda, b_w_in, b_kv_norm_g, b_w_uk, b_w_uv, b_idx_norm_g, b_idx_norm_b, w_mem_kv, w_out, ln1_g, ln1_b, router_w, router_b, exp_w1, exp_b1, exp_w2, exp_b2, ln2_g, ln2_b):
    bsz, seq, d = x.shape
    tiles = _tiles(seq)
    ts = tiles["ts"]
    row = lambda v: v[None, :]

    def finish_layer(layer, x, tok, mq):
        kv = _matmul(mem.reshape(-1, d), w_mem_kv[layer].astype(BF16), mem.shape[1])
        kv = kv.astype(BF16).reshape(bsz, mem.shape[1], -1)
        x1 = _post_mixer(x, tok, mq, kv, w_out[layer].astype(BF16), row(ln1_g[layer]),
                         row(ln1_b[layer]), ts)
        x2 = _moe_layer(x1.reshape(-1, d), router_w[layer], router_b[layer], exp_w1[layer],
                        exp_b1[layer], exp_w2[layer], exp_b2[layer], ln2_g[layer], ln2_b[layer],
                        tiles)
        return x2.reshape(bsz, seq, d)

    tok, mq = _rglru_front(x, a_w_in[0].astype(BF16), a_conv_w[0], row(a_conv_b[0]),
                           _block_diag(a_wr[0]).astype(BF16), row(a_br[0]),
                           _block_diag(a_wi[0]).astype(BF16), row(a_bi[0]), row(a_lambda[0]), ts)
    x = finish_layer(0, x, tok, mq)

    nh = b_w_uk.shape[2]
    tok_w = nh * HEAD_DIM
    q0, c0, iq0, mq0, ik0, iw0, total = _dsa_layout(tok_w)
    w = b_w_in[0]
    o_c, o_iq = tok_w, tok_w + KV_LATENT
    o_ik = o_iq + IDX_HEADS * IDX_DIM
    o_iw = o_ik + IDX_DIM
    o_mq = o_iw + IDX_HEADS
    w_re = jnp.zeros((d, total), F32)
    w_re = w_re.at[:, q0:q0 + tok_w].set(w[:, :tok_w])
    w_re = w_re.at[:, c0:c0 + KV_LATENT].set(w[:, o_c:o_iq])
    w_re = w_re.at[:, iq0:iq0 + IDX_HEADS * IDX_DIM].set(w[:, o_iq:o_ik])
    w_re = w_re.at[:, ik0:ik0 + IDX_DIM].set(w[:, o_ik:o_iw])
    w_re = w_re.at[:, iw0:iw0 + IDX_HEADS].set(w[:, o_iw:o_mq])
    w_re = w_re.at[:, mq0:mq0 + MEM_WIDTH].set(w[:, o_mq:])
    wuk = jnp.transpose(b_w_uk[0], (1, 2, 0)).astype(BF16)
    wuv = jnp.transpose(b_w_uv[0], (1, 0, 2))
    wuv2 = jnp.zeros((nh // 2, 2 * KV_LATENT, 2 * HEAD_DIM), F32)
    wuv2 = wuv2.at[:, :KV_LATENT, :HEAD_DIM].set(wuv[0::2])
    wuv2 = wuv2.at[:, KV_LATENT:, HEAD_DIM:].set(wuv[1::2]).astype(BF16)
    qlat, c, iq, ik, iw, mq = _dsa_proj(x, w_re.astype(BF16), wuk, row(b_kv_norm_g[0]),
                                        row(b_idx_norm_g[0]), row(b_idx_norm_b[0]), ts)
    k_sel = min(TOPK_MAX, seq // 4)
    chunk = tiles["chunk"]
    ikt = jnp.swapaxes(ik.reshape(bsz, MASK_BITS, chunk, IDX_DIM), 2, 3)
    mask = _dsa_select(iq, iw, ikt, chunk, k_sel)
    tok = _dsa_attend(qlat, c, mask, _bias_tiles(rel_bias, tiles["tq_att"], chunk), wuv2,
                      tiles["tq_att"], chunk)
    x = finish_layer(1, x, tok, mq)
    return x
```

```python
import functools
import math

import jax
import jax.numpy as jnp
import numpy as np
from jax import lax
from jax.experimental import pallas as pl
from jax.experimental.pallas import tpu as pltpu

F32 = jnp.float32
BF16 = jnp.bfloat16
I32 = jnp.int32

HEAD_DIM = 64
MEM_HEADS = 4
MEM_WIDTH = MEM_HEADS * HEAD_DIM
LRU_C = 8.0
CONV_W = 4
KV_LATENT = 128
IDX_HEADS = 4
IDX_DIM = 64
TOPK_MAX = 256
REL_BUCKETS = 32
REL_MAX_DIST = 128
N_EXPERTS = 32
TOP_K = 4
SWIGLU_LIMIT = 7.0
SWIGLU_ALPHA = 1.702
DEPTH = 2
DN_ALPHA = (2 * DEPTH) ** 0.25
LN_EPS = 1e-5
RMS_EPS = 1e-6

SUBLANES = 8
LANES = 128
MASK_BITS = 32
INT_MIN = -2 ** 31
NEG_BIG = -1e30
VMEM_LIMIT = 56 * 1024 * 1024


def _tiles(seq):
    chunk = seq // MASK_BITS
    return dict(
        ts=min(512, seq),
        chunk=chunk,
        tq_att=2 * chunk,
        tt=256,
        tm=512,
        tr=512,
    )


def _dot(a, b):
    return jnp.dot(a, b, preferred_element_type=F32)


def _dot_nt(a, b):
    return lax.dot_general(a, b, (((1,), (1,)), ((), ())), preferred_element_type=F32)


def _sigmoid(x):
    return 1.0 / (1.0 + jnp.exp(-x))


def _layer_norm(z, g, b):
    mu = jnp.mean(z, axis=-1, keepdims=True)
    zc = z - mu
    var = jnp.mean(zc * zc, axis=-1, keepdims=True)
    return zc * lax.rsqrt(var + LN_EPS) * g + b


def _params(*sem):
    return pltpu.CompilerParams(dimension_semantics=sem, vmem_limit_bytes=VMEM_LIMIT)


def _mm_kernel(a_ref, b_ref, o_ref):
    o_ref[...] = _dot(a_ref[...].astype(BF16), b_ref[...])


def _matmul(a, b, tm):
    m, k = a.shape
    n = b.shape[1]
    return pl.pallas_call(
        _mm_kernel,
        grid=(m // tm,),
        in_specs=[pl.BlockSpec((tm, k), lambda i: (i, 0)),
                  pl.BlockSpec((k, n), lambda i: (0, 0))],
        out_specs=pl.BlockSpec((tm, n), lambda i: (i, 0)),
        out_shape=jax.ShapeDtypeStruct((m, n), F32),
        compiler_params=_params("arbitrary"),
        name="matmul",
    )(a, b)


def _rglru_kernel(x_ref, win_ref, cw_ref, cb_ref, wr_ref, br_ref, wi_ref, bi_ref, lam_ref,
                  tok_ref, mq_ref, tail_ref, h_ref, a_s, u_s, *, tok_w):
    s = pl.program_id(1)

    @pl.when(s == 0)
    def _():
        tail_ref[...] = jnp.zeros_like(tail_ref)
        h_ref[...] = jnp.zeros_like(h_ref)

    proj = _dot(x_ref[0].astype(BF16), win_ref[...])
    xb = proj[:, :tok_w]
    gb = proj[:, tok_w:2 * tok_w]
    mq_ref[0] = proj[:, 2 * tok_w:]
    ts = xb.shape[0]

    xe = jnp.concatenate([tail_ref[...], xb], axis=0)
    tail_ref[...] = xb[ts - SUBLANES:, :]
    xc = xb * cw_ref[CONV_W - 1:CONV_W, :] + cb_ref[...]
    for d in range(1, CONV_W):
        sh = pltpu.roll(xe, d, 0)[SUBLANES:, :]
        xc = xc + sh * cw_ref[CONV_W - 1 - d:CONV_W - d, :]

    xcb = xc.astype(BF16)
    r = _sigmoid(_dot(xcb, wr_ref[...]) + br_ref[...])
    ig = _sigmoid(_dot(xcb, wi_ref[...]) + bi_ref[...])
    nl = -lam_ref[...]
    softplus = jnp.maximum(nl, 0.0) + jnp.log1p(jnp.exp(-jnp.abs(nl)))
    log_a = (-LRU_C) * r * softplus
    a = jnp.exp(log_a)
    th = jnp.tanh(log_a)
    u = jnp.sqrt(-2.0 * th / (1.0 - th)) * (ig * xc)

    row = lax.broadcasted_iota(I32, a.shape, 0) & (SUBLANES - 1)
    for d in (1, 2, 4):
        m = row >= d
        u = jnp.where(m, a * pltpu.roll(u, d, 0) + u, u)
        a = jnp.where(m, a * pltpu.roll(a, d, 0), a)
    a_s[...] = a
    u_s[...] = u

    def group(g, h):
        sl = pl.ds(pl.multiple_of(g * SUBLANES, SUBLANES), SUBLANES)
        hg = u_s[sl, :] + a_s[sl, :] * h
        u_s[sl, :] = hg
        return jnp.broadcast_to(hg[SUBLANES - 1:, :], hg.shape)

    h_ref[...] = lax.fori_loop(0, ts // SUBLANES, group, h_ref[...])

    gelu = 0.5 * gb * (1.0 + jnp.tanh(math.sqrt(2.0 / math.pi) * (gb + 0.044715 * (gb * gb * gb))))
    tok_ref[0] = u_s[...] * gelu


def _rglru_front(x, w_in, conv_w, conv_b, wr, br, wi, bi, lam, ts):
    b, s, d = x.shape
    tok_w = conv_w.shape[-1]
    mem_w = w_in.shape[-1] - 2 * tok_w
    const = lambda shape: pl.BlockSpec(shape, lambda i, j: (0,) * len(shape))
    return pl.pallas_call(
        functools.partial(_rglru_kernel, tok_w=tok_w),
        grid=(b, s // ts),
        in_specs=[pl.BlockSpec((1, ts, d), lambda i, j: (i, j, 0)),
                  const(w_in.shape), const(conv_w.shape), const(conv_b.shape),
                  const(wr.shape), const(br.shape), const(wi.shape), const(bi.shape),
                  const(lam.shape)],
        out_specs=[pl.BlockSpec((1, ts, tok_w), lambda i, j: (i, j, 0)),
                   pl.BlockSpec((1, ts, mem_w), lambda i, j: (i, j, 0))],
        out_shape=[jax.ShapeDtypeStruct((b, s, tok_w), F32),
                   jax.ShapeDtypeStruct((b, s, mem_w), F32)],
        scratch_shapes=[pltpu.VMEM((SUBLANES, tok_w), F32), pltpu.VMEM((SUBLANES, tok_w), F32),
                        pltpu.VMEM((ts, tok_w), F32), pltpu.VMEM((ts, tok_w), F32)],
        compiler_params=_params("arbitrary", "arbitrary"),
        name="rglru_front",
    )(x, w_in, conv_w, conv_b, wr, br, wi, bi, lam)


def _dsa_layout(tok_w):
    q0 = 0
    c0 = q0 + tok_w
    iq0 = c0 + KV_LATENT
    mq0 = iq0 + IDX_HEADS * IDX_DIM
    ik0 = mq0 + MEM_WIDTH
    iw0 = ik0 + LANES
    total = iw0 + LANES
    return q0, c0, iq0, mq0, ik0, iw0, total


def _dsa_proj_kernel(x_ref, w_ref, wuk_ref, kvg_ref, ig_ref, ib_ref,
                     qlat_ref, c_ref, iq_ref, ik_ref, iw_ref, mq_ref, *, tok_w):
    q0, c0, iq0, mq0, ik0, iw0, _ = _dsa_layout(tok_w)
    proj = _dot(x_ref[0].astype(BF16), w_ref[...])
    for h in range(tok_w // HEAD_DIM):
        qh = proj[:, q0 + h * HEAD_DIM:q0 + (h + 1) * HEAD_DIM].astype(BF16)
        ql = _dot(qh, wuk_ref[h]) * HEAD_DIM ** -0.5
        qlat_ref[0, h] = ql.astype(BF16)
    c = proj[:, c0:c0 + KV_LATENT]
    c = c * lax.rsqrt(jnp.mean(c * c, axis=-1, keepdims=True) + RMS_EPS) * kvg_ref[...]
    c_ref[0] = c.astype(BF16)
    iq_ref[0] = proj[:, iq0:iq0 + IDX_HEADS * IDX_DIM].astype(BF16)
    ik = _layer_norm(proj[:, ik0:ik0 + IDX_DIM], ig_ref[...], ib_ref[...])
    ik_ref[0] = ik.astype(BF16)
    iw_ref[0] = proj[:, iw0:iw0 + IDX_HEADS] * (IDX_HEADS ** -0.5 * IDX_DIM ** -0.5)
    mq_ref[0] = proj[:, mq0:mq0 + MEM_WIDTH]


def _dsa_proj(x, w, wuk, kvg, ig, ib, ts):
    b, s, d = x.shape
    nh = wuk.shape[0]
    tok_w = nh * HEAD_DIM
    const = lambda shape: pl.BlockSpec(shape, lambda i, j: (0,) * len(shape))
    tile = lambda w_: pl.BlockSpec((1, ts, w_), lambda i, j: (i, j, 0))
    widths = [KV_LATENT, IDX_HEADS * IDX_DIM, IDX_DIM, IDX_HEADS, MEM_WIDTH]
    dtypes = [BF16, BF16, BF16, F32, F32]
    return pl.pallas_call(
        functools.partial(_dsa_proj_kernel, tok_w=tok_w),
        grid=(b, s // ts),
        in_specs=[tile(d), const(w.shape), const(wuk.shape), const(kvg.shape),
                  const(ig.shape), const(ib.shape)],
        out_specs=[pl.BlockSpec((1, nh, ts, KV_LATENT), lambda i, j: (i, 0, j, 0))]
        + [tile(w_) for w_ in widths],
        out_shape=[jax.ShapeDtypeStruct((b, nh, s, KV_LATENT), BF16)]
        + [jax.ShapeDtypeStruct((b, s, w_), t) for w_, t in zip(widths, dtypes)],
        compiler_params=_params("arbitrary", "arbitrary"),
        name="dsa_proj",
    )(x, w, wuk, kvg, ig, ib)


def _row_popcount(words):
    pc = lax.population_count(words)
    part = pc[:, :LANES]
    for i in range(1, words.shape[1] // LANES):
        part = part + pc[:, i * LANES:(i + 1) * LANES]
    return jnp.sum(part.astype(F32), axis=1, keepdims=True)


def _dsa_select_kernel(iq_ref, iw_ref, ikt_ref, mask_ref, keys_ref, iwb_ref, tie_ref,
                       *, k_sel, chunk):
    qi = pl.program_id(1)
    tq = iq_ref.shape[1]
    iq = iq_ref[0]
    iw = iw_ref[0]
    for h in range(IDX_HEADS):
        iwb_ref[h] = jnp.broadcast_to(iw[:, h:h + 1], (tq, chunk))
    iqh = [iq[:, h * IDX_DIM:(h + 1) * IDX_DIM] for h in range(IDX_HEADS)]

    def ukey(c):
        ikc = ikt_ref[0, c]
        sc = jnp.maximum(_dot(iqh[0], ikc), 0.0) * iwb_ref[0]
        for h in range(1, IDX_HEADS):
            sc = sc + jnp.maximum(_dot(iqh[h], ikc), 0.0) * iwb_ref[h]
        bits = lax.bitcast_convert_type(sc, I32)
        word = bits ^ ((bits >> 31) | INT_MIN)
        return word + (word == 0x7FFFFFFF).astype(I32)

    def score(c, _):
        keys_ref[c] = ukey(c)
        return 0

    lax.fori_loop(0, qi, score, 0)
    row = lax.broadcasted_iota(I32, (tq, chunk), 0)
    lane = lax.broadcasted_iota(I32, (tq, chunk), 1)
    keys_ref[qi] = jnp.where(lane <= row, ukey(qi), 0)

    def clear(c, _):
        keys_ref[c] = jnp.zeros((tq, chunk), I32)
        return 0

    lax.fori_loop(qi + 1, MASK_BITS, clear, 0)

    def transpose(g, _):
        rows = pl.ds(pl.multiple_of(g * SUBLANES, SUBLANES), SUBLANES)
        for half in range(chunk // LANES):
            cols = slice(half * LANES, (half + 1) * LANES)
            a = [keys_ref[c, rows, cols] for c in range(MASK_BITS)]
            j, m = MASK_BITS // 2, 0x0000FFFF
            while j:
                k = 0
                while k < MASK_BITS:
                    t = (lax.shift_right_logical(a[k], j) ^ a[k + j]) & m
                    a[k] = a[k] ^ (t << j)
                    a[k + j] = a[k + j] ^ t
                    k = (k + j + 1) & ~j
                j >>= 1
                m = (m ^ (m << j)) & 0xFFFFFFFF
                m = m - (1 << 32) if m >= (1 << 31) else m
            for c in range(MASK_BITS):
                keys_ref[c, rows, cols] = a[c]
        return 0

    lax.fori_loop(0, tq // SUBLANES, transpose, 0)

    def plane_step(i, carry):
        active, greater, need, any_one = carry
        ones = active & keys_ref[MASK_BITS - 1 - i]
        cnt = _row_popcount(ones)
        take = cnt >= need
        active = jnp.where(take, ones, active ^ ones)
        greater = jnp.where(take, greater, greater | ones)
        need = jnp.where(take, need, need - cnt)
        return active, greater, need, jnp.where(take, 1.0, any_one)

    full = jnp.full((tq, chunk), -1, I32)
    ties, greater, need, any_one = lax.fori_loop(
        0, MASK_BITS, plane_step,
        (full, jnp.zeros((tq, chunk), I32), jnp.full((tq, 1), float(k_sel), F32),
         jnp.zeros((tq, 1), F32)))
    ties = jnp.where(any_one > 0.0, ties, 0)
    tie_ref[...] = ties

    @pl.when(jnp.max(_row_popcount(ties) - need) > 0.0)
    def _():
        shift = chunk.bit_length() - 1
        lane_i = lax.broadcasted_iota(I32, (tq, chunk), 1)

        def below(v, inclusive):
            vh = jnp.broadcast_to(v >> shift, (tq, chunk))
            vl = jnp.broadcast_to(v & (chunk - 1), (tq, chunk))
            top = jnp.int32(1) << vh
            edge = (lane_i <= vl) if inclusive else (lane_i < vl)
            return (top - 1) | jnp.where(edge, top, 0)

        nbits = (MASK_BITS * chunk).bit_length() - 1

        def idx_step(i, j):
            cand = j | (jnp.int32(1) << (nbits - 1 - i))
            cnt = _row_popcount(ties & below(cand, False))
            return jnp.where(cnt < need, cand, j)

        last = lax.fori_loop(0, nbits, idx_step, jnp.zeros((tq, 1), I32))
        tie_ref[...] = ties & below(last, True)

    mask_ref[0] = greater | tie_ref[...]


def _dsa_select(iq, iw, ikt, chunk, k_sel):
    b, s, _ = iq.shape
    tq = chunk
    return pl.pallas_call(
        functools.partial(_dsa_select_kernel, k_sel=k_sel, chunk=chunk),
        grid=(b, s // tq),
        in_specs=[pl.BlockSpec((1, tq, iq.shape[2]), lambda i, j: (i, j, 0)),
                  pl.BlockSpec((1, tq, iw.shape[2]), lambda i, j: (i, j, 0)),
                  pl.BlockSpec((1,) + ikt.shape[1:], lambda i, j: (i, 0, 0, 0))],
        out_specs=pl.BlockSpec((1, tq, chunk), lambda i, j: (i, j, 0)),
        out_shape=jax.ShapeDtypeStruct((b, s, chunk), I32),
        scratch_shapes=[pltpu.VMEM((MASK_BITS, tq, chunk), I32),
                        pltpu.VMEM((IDX_HEADS, tq, chunk), F32),
                        pltpu.VMEM((tq, chunk), I32)],
        compiler_params=_params("arbitrary", "arbitrary"),
        name="dsa_select",
    )(iq, iw, ikt)


def _dsa_attend_kernel(qi_ref, kb_ref, qlat_ref, c_ref, mask_ref, bias_ref, wuv_ref, out_ref,
                       m_ref, acc_ref, madd_ref, *, nh, rg):
    p_id = pl.program_id(1)
    qi = qi_ref[p_id]
    kb = kb_ref[p_id]
    tq, tk = madd_ref.shape
    ngroups = tq // rg
    qmul = tq // tk
    first_near = qmul * qi - 1

    @pl.when(kb == 0)
    def _():
        m_ref[...] = jnp.full(m_ref.shape, NEG_BIG, F32)
        acc_ref[...] = jnp.zeros_like(acc_ref)

    madd_ref[...] = jnp.where(((mask_ref[0] >> kb) & 1) == 1, 0.0, NEG_BIG)
    c = c_ref[0]
    c_one = jnp.concatenate([c, jnp.ones_like(c)], axis=1)

    def heads(with_bias):
        for i in range(nh * ngroups):
            h, g = divmod(i, ngroups)
            rows = pl.ds(i * rg, rg)
            qrows = pl.ds(g * rg, rg)
            lg = _dot_nt(qlat_ref[0, h, qrows, :], c) + madd_ref[qrows, :]
            if with_bias:
                lg = lg + bias_ref[0, h, qrows, :]
            m_old = m_ref[rows, :]
            m_new = jnp.maximum(m_old, jnp.max(lg, axis=1, keepdims=True))
            alpha = jnp.exp(m_old - m_new)
            p = jnp.exp(lg - jnp.concatenate([m_new] * (tk // LANES), axis=1))
            m_ref[rows, :] = m_new
            acc_ref[rows, :] = (jnp.concatenate([alpha, alpha], axis=1) * acc_ref[rows, :]
                                + _dot(p.astype(BF16), c_one))

    @pl.when(kb >= first_near)
    def _():
        heads(True)

    @pl.when(kb < first_near)
    def _():
        heads(False)

    @pl.when(kb == first_near + qmul)
    def _():
        for j in range(nh // 2):
            r0, r1 = pl.ds(2 * j * tq, tq), pl.ds((2 * j + 1) * tq, tq)
            o = jnp.concatenate([acc_ref[r0, :KV_LATENT] / acc_ref[r0, KV_LATENT:],
                                 acc_ref[r1, :KV_LATENT] / acc_ref[r1, KV_LATENT:]], axis=1)
            out_ref[0, :, j * LANES:(j + 1) * LANES] = _dot(o.astype(BF16), wuv_ref[j])


def _dsa_attend(qlat, c, mask, bias, wuv2, tq, tk):
    b, nh, s, _ = qlat.shape
    nq = s // tq
    qmul = tq // tk
    rg = min(128, tq)
    qi_tab = np.concatenate([np.full(qmul * (q + 1), q, np.int32) for q in range(nq)])
    kb_tab = np.concatenate([np.arange(qmul * (q + 1), dtype=np.int32) for q in range(nq)])
    grid_spec = pltpu.PrefetchScalarGridSpec(
        num_scalar_prefetch=2,
        grid=(b, len(qi_tab)),
        in_specs=[pl.BlockSpec((1, nh, tq, KV_LATENT), lambda i, p, qt, kt: (i, 0, qt[p], 0)),
                  pl.BlockSpec((1, tk, c.shape[2]), lambda i, p, qt, kt: (i, kt[p], 0)),
                  pl.BlockSpec((1, tq, mask.shape[2]), lambda i, p, qt, kt: (i, qt[p], 0)),
                  pl.BlockSpec((1, nh, tq, tk),
                               lambda i, p, qt, kt: (jnp.clip(kt[p] - (qmul * qt[p] - 1), 0, qmul),
                                                     0, 0, 0)),
                  pl.BlockSpec(wuv2.shape, lambda i, p, qt, kt: (0, 0, 0))],
        out_specs=pl.BlockSpec((1, tq, nh * HEAD_DIM), lambda i, p, qt, kt: (i, qt[p], 0)),
        scratch_shapes=[pltpu.VMEM((nh * tq, LANES), F32),
                        pltpu.VMEM((nh * tq, 2 * KV_LATENT), F32),
                        pltpu.VMEM((tq, tk), F32)],
    )
    return pl.pallas_call(
        functools.partial(_dsa_attend_kernel, nh=nh, rg=rg),
        grid_spec=grid_spec,
        out_shape=jax.ShapeDtypeStruct((b, s, nh * HEAD_DIM), F32),
        compiler_params=_params("arbitrary", "arbitrary"),
        name="dsa_attend",
    )(jnp.asarray(qi_tab), jnp.asarray(kb_tab), qlat, c, mask, bias, wuv2)


def _t5_bucket(rel):
    n = jnp.maximum(rel, 0)
    max_exact = REL_BUCKETS // 2
    large = max_exact + (jnp.log(jnp.maximum(n, 1).astype(F32) / max_exact)
                         / math.log(REL_MAX_DIST / max_exact) * (REL_BUCKETS - max_exact)).astype(I32)
    large = jnp.minimum(large, REL_BUCKETS - 1)
    return jnp.where(n < max_exact, n, large)


def _bias_tiles(rel_bias, tq, tk):
    assert tk >= REL_MAX_DIST
    rel0 = jnp.arange(tq, dtype=I32)[:, None] - jnp.arange(tk, dtype=I32)[None, :]
    shifted = rel_bias - rel_bias[REL_BUCKETS - 1]
    buckets = jnp.stack([_t5_bucket(rel0 + (1 - j) * tk) for j in range(tq // tk + 1)])
    onehot = jax.nn.one_hot(buckets, REL_BUCKETS, dtype=F32)
    return jnp.einsum('dijb,bh->dhij', onehot, shifted, precision=lax.Precision.HIGHEST)


def _post_kernel(x_ref, tok_ref, mq_ref, kv_ref, wout_ref, g_ref, b_ref, x1_ref, *, tok_w):
    mq = mq_ref[0]
    kv = kv_ref[0]
    y = _dot(tok_ref[0].astype(BF16), wout_ref[:tok_w, :])
    for h in range(MEM_HEADS):
        q = mq[:, h * HEAD_DIM:(h + 1) * HEAD_DIM].astype(BF16)
        k = kv[:, h * HEAD_DIM:(h + 1) * HEAD_DIM]
        v = kv[:, MEM_WIDTH + h * HEAD_DIM:MEM_WIDTH + (h + 1) * HEAD_DIM]
        lg = _dot_nt(q, k) * HEAD_DIM ** -0.5
        p = jnp.exp(lg - jnp.max(lg, axis=1, keepdims=True)).astype(BF16)
        o = _dot(p, v) / _dot(p, jnp.ones((p.shape[1], LANES), BF16))[:, :HEAD_DIM]
        y = y + _dot(o.astype(BF16), wout_ref[tok_w + h * HEAD_DIM:tok_w + (h + 1) * HEAD_DIM, :])
    x1_ref[0] = _layer_norm(DN_ALPHA * x_ref[0] + y, g_ref[...], b_ref[...])


def _post_mixer(x, tok, mq, kv, w_out, g, b, ts):
    bsz, s, d = x.shape
    tok_w = tok.shape[2]
    const = lambda shape: pl.BlockSpec(shape, lambda i, j: (0,) * len(shape))
    tile = lambda w_: pl.BlockSpec((1, ts, w_), lambda i, j: (i, j, 0))
    return pl.pallas_call(
        functools.partial(_post_kernel, tok_w=tok_w),
        grid=(bsz, s // ts),
        in_specs=[tile(d), tile(tok_w), tile(mq.shape[2]),
                  pl.BlockSpec((1,) + kv.shape[1:], lambda i, j: (i, 0, 0)),
                  const(w_out.shape), const(g.shape), const(b.shape)],
        out_specs=tile(d),
        out_shape=jax.ShapeDtypeStruct((bsz, s, d), F32),
        compiler_params=_params("arbitrary", "arbitrary"),
        name="post_mixer",
    )(x, tok, mq, kv, w_out, g, b)


def _router_kernel(x_ref, wh_ref, wl_ref, rb_ref, e_ref, gate_ref, rank_ref, cnt_ref, carry_ref):
    @pl.when(pl.program_id(0) == 0)
    def _():
        carry_ref[...] = jnp.zeros_like(carry_ref)

    x = x_ref[...]
    tr = x.shape[0]
    xh = x.astype(BF16)
    xl = (x - xh.astype(F32)).astype(BF16)
    lg = _dot(xh, wh_ref[...]) + (_dot(xl, wh_ref[...]) + _dot(xh, wl_ref[...])) + rb_ref[...]
    lane = lax.broadcasted_iota(I32, lg.shape, 1).astype(F32)
    lg = jnp.where(lane < N_EXPERTS, lg, -jnp.inf)

    idx, val = [], []
    onehot = jnp.zeros(lg.shape, F32)
    for _ in range(TOP_K):
        m = jnp.max(lg, axis=1, keepdims=True)
        i = jnp.min(jnp.where(lg == m, lane, float(LANES)), axis=1, keepdims=True)
        hit = lane == i
        idx.append(i)
        val.append(m)
        onehot = jnp.where(hit, 1.0, onehot)
        lg = jnp.where(hit, -jnp.inf, lg)

    ex = [jnp.exp(v - val[0]) for v in val]
    den = ex[0] + ex[1] + ex[2] + ex[3]

    r_i = lax.broadcasted_iota(I32, (tr, tr), 0)
    c_i = lax.broadcasted_iota(I32, (tr, tr), 1)
    tri = jnp.where(c_i < r_i, 1.0, 0.0).astype(BF16)
    before = _dot(tri, onehot.astype(BF16)) + carry_ref[...]
    carry_ref[...] = carry_ref[...] + jnp.sum(onehot, axis=0, keepdims=True)
    cnt_ref[...] = carry_ref[...]

    lane_k = lax.broadcasted_iota(I32, (tr, TOP_K), 1)
    e_out = jnp.zeros((tr, TOP_K), I32)
    g_out = jnp.zeros((tr, TOP_K), F32)
    r_out = jnp.zeros((tr, TOP_K), I32)
    for k in range(TOP_K):
        rk = jnp.sum(jnp.where(lane == idx[k], before, 0.0), axis=1, keepdims=True).astype(I32)
        e_out = jnp.where(lane_k == k, idx[k].astype(I32), e_out)
        g_out = jnp.where(lane_k == k, ex[k] / den, g_out)
        r_out = jnp.where(lane_k == k, rk, r_out)
    e_ref[...] = e_out
    gate_ref[...] = g_out
    rank_ref[...] = r_out


def _router(x, wh, wl, rb, tr):
    t, d = x.shape
    const = lambda shape: pl.BlockSpec(shape, lambda i: (0,) * len(shape))
    small = pl.BlockSpec((tr, TOP_K), lambda i: (i, 0))
    return pl.pallas_call(
        _router_kernel,
        grid=(t // tr,),
        in_specs=[pl.BlockSpec((tr, d), lambda i: (i, 0)), const(wh.shape), const(wl.shape),
                  const(rb.shape)],
        out_specs=[small, small, small, const((1, LANES))],
        out_shape=[jax.ShapeDtypeStruct((t, TOP_K), I32), jax.ShapeDtypeStruct((t, TOP_K), F32),
                   jax.ShapeDtypeStruct((t, TOP_K), I32), jax.ShapeDtypeStruct((1, LANES), F32)],
        scratch_shapes=[pltpu.VMEM((1, LANES), F32)],
        compiler_params=_params("arbitrary"),
        name="router",
    )(x, wh, wl, rb)


def _dest_kernel(e_ref, rank_ref, start_ref, dest_ref):
    e = e_ref[...]
    lane = lax.broadcasted_iota(I32, (e.shape[0], LANES), 1)
    lane_k = lax.broadcasted_iota(I32, e.shape, 1)
    out = rank_ref[...]
    for k in range(TOP_K):
        st = jnp.sum(jnp.where(lane == e[:, k:k + 1], start_ref[...], 0.0), axis=1, keepdims=True)
        out = out + jnp.where(lane_k == k, st.astype(I32), 0)
    dest_ref[...] = out


def _dest(e, rank, start, tr):
    t = e.shape[0]
    small = pl.BlockSpec((tr, TOP_K), lambda i: (i, 0))
    return pl.pallas_call(
        _dest_kernel,
        grid=(t // tr,),
        in_specs=[small, small, pl.BlockSpec((1, LANES), lambda i: (0, 0))],
        out_specs=small,
        out_shape=jax.ShapeDtypeStruct((t, TOP_K), I32),
        compiler_params=_params("arbitrary"),
        name="dest",
    )(e, rank, start)


def _dispatch_kernel(dest_ref, meta_ref, x_ref, xs_ref, zero_ref, sem, *, tm):
    tt = x_ref.shape[0]
    ntail = (xs_ref.shape[0] - meta_ref[2 * N_EXPERTS]) // tm

    def pad_copy(e, r):
        row = meta_ref[N_EXPERTS + e] + r
        return pltpu.make_async_copy(zero_ref.at[pl.ds(0, 1)], xs_ref.at[pl.ds(row, 1)], sem)

    def tail_copy(j):
        row = pl.multiple_of(meta_ref[2 * N_EXPERTS] + j * tm, tm)
        return pltpu.make_async_copy(zero_ref, xs_ref.at[pl.ds(row, tm)], sem)

    def for_each_pad(fn):
        def per_expert(e, _):
            width = meta_ref[N_EXPERTS + e + 1] - meta_ref[N_EXPERTS + e]
            lax.fori_loop(meta_ref[e], width, lambda r, c: (fn(pad_copy(e, r)), c)[1], 0)
            return 0
        lax.fori_loop(0, N_EXPERTS, per_expert, 0)
        lax.fori_loop(0, ntail, lambda j, c: (fn(tail_copy(j)), c)[1], 0)

    @pl.when(pl.program_id(0) == 0)
    def _():
        zero_ref[...] = jnp.zeros_like(zero_ref)
        for_each_pad(lambda cp: cp.start())
        for_each_pad(lambda cp: cp.wait())

    def row_copy(i, k):
        d = dest_ref[i * TOP_K + k]
        return pltpu.make_async_copy(x_ref.at[pl.ds(i, 1)], xs_ref.at[pl.ds(d, 1)], sem)

    def issue(i, _):
        for k in range(TOP_K):
            row_copy(i, k).start()
        return 0

    def drain(i, _):
        for k in range(TOP_K):
            row_copy(i, k).wait()
        return 0

    lax.fori_loop(0, tt, issue, 0, unroll=8)
    lax.fori_loop(0, tt, drain, 0, unroll=8)


def _dispatch(dest_flat, meta, x, rows, tt, tm):
    t, d = x.shape
    return pl.pallas_call(
        functools.partial(_dispatch_kernel, tm=tm),
        grid=(t // tt,),
        in_specs=[pl.BlockSpec((tt * TOP_K,), lambda i: (i,), memory_space=pltpu.SMEM),
                  pl.BlockSpec(memory_space=pltpu.SMEM),
                  pl.BlockSpec((tt, d), lambda i: (i, 0))],
        out_specs=pl.BlockSpec(memory_space=pl.ANY),
        out_shape=jax.ShapeDtypeStruct((rows, d), x.dtype),
        scratch_shapes=[pltpu.VMEM((tm, d), x.dtype), pltpu.SemaphoreType.DMA(())],
        compiler_params=_params("arbitrary"),
        name="dispatch",
    )(dest_flat, meta, x)


def _ffn_kernel(be_ref, xs_ref, w1_ref, b1_ref, w2_ref, b2_ref, ys_ref):
    del be_ref
    de = w2_ref.shape[2]
    h = _dot(xs_ref[...].astype(BF16), w1_ref[0, 0]) + b1_ref[0, 0]
    gt = jnp.minimum(h[:, :de], SWIGLU_LIMIT)
    up = jnp.clip(h[:, de:], -SWIGLU_LIMIT, SWIGLU_LIMIT)
    act = (up + 1.0) * (gt * _sigmoid(SWIGLU_ALPHA * gt))
    ys_ref[...] = _dot(act.astype(BF16), w2_ref[0, 0]) + b2_ref[0, 0]


def _expert_ffn(blk_e, xs, w1, b1, w2, b2, layer, tm):
    rows, d = xs.shape
    expert = lambda a: pl.BlockSpec((1, 1) + a.shape[2:], lambda i, be: (layer, be[i], 0, 0))
    grid_spec = pltpu.PrefetchScalarGridSpec(
        num_scalar_prefetch=1,
        grid=(rows // tm,),
        in_specs=[pl.BlockSpec((tm, d), lambda i, be: (i, 0)),
                  expert(w1), expert(b1), expert(w2), expert(b2)],
        out_specs=pl.BlockSpec((tm, d), lambda i, be: (i, 0)),
    )
    return pl.pallas_call(
        _ffn_kernel,
        grid_spec=grid_spec,
        out_shape=jax.ShapeDtypeStruct((rows, d), F32),
        compiler_params=_params("arbitrary"),
        name="expert_ffn",
    )(blk_e, xs, w1, b1, w2, b2)


def _combine_kernel(dest_ref, dest_next_ref, x_ref, gate_ref, g_ref, b_ref, ys_ref, out_ref,
                    buf, sem):
    tt = x_ref.shape[0]
    step = pl.program_id(0)
    slot = step % 2

    def row_copy(idx_ref, to, i, k):
        d = idx_ref[i * TOP_K + k]
        return pltpu.make_async_copy(ys_ref.at[pl.ds(d, 1)], buf.at[to, k, pl.ds(i, 1)],
                                     sem.at[to])

    def issue(idx_ref, to):
        def body(i, _):
            for k in range(TOP_K):
                row_copy(idx_ref, to, i, k).start()
            return 0
        lax.fori_loop(0, tt, body, 0, unroll=8)

    @pl.when(step == 0)
    def _():
        issue(dest_ref, slot)

    @pl.when(step + 1 < pl.num_programs(0))
    def _():
        issue(dest_next_ref, 1 - slot)

    def drain(i, _):
        for k in range(TOP_K):
            row_copy(dest_ref, slot, i, k).wait()
        return 0

    lax.fori_loop(0, tt, drain, 0, unroll=8)

    gate = gate_ref[...]
    ffn = gate[:, 0:1] * buf[slot, 0]
    for k in range(1, TOP_K):
        ffn = ffn + gate[:, k:k + 1] * buf[slot, k]
    out_ref[...] = _layer_norm(DN_ALPHA * x_ref[...] + ffn, g_ref[...], b_ref[...])


def _combine(dest_flat, x, gate, g, b, ys, tt):
    t, d = x.shape
    const = lambda shape: pl.BlockSpec(shape, lambda i: (0,) * len(shape))
    nstep = t // tt
    return pl.pallas_call(
        _combine_kernel,
        grid=(nstep,),
        in_specs=[pl.BlockSpec((tt * TOP_K,), lambda i: (i,), memory_space=pltpu.SMEM),
                  pl.BlockSpec((tt * TOP_K,), lambda i: (jnp.minimum(i + 1, nstep - 1),),
                               memory_space=pltpu.SMEM),
                  pl.BlockSpec((tt, d), lambda i: (i, 0)),
                  pl.BlockSpec((tt, TOP_K), lambda i: (i, 0)),
                  const(g.shape), const(b.shape),
                  pl.BlockSpec(memory_space=pl.ANY)],
        out_specs=pl.BlockSpec((tt, d), lambda i: (i, 0)),
        out_shape=jax.ShapeDtypeStruct((t, d), F32),
        scratch_shapes=[pltpu.VMEM((2, TOP_K, tt, d), F32), pltpu.SemaphoreType.DMA((2,))],
        compiler_params=_params("arbitrary"),
        name="combine",
    )(dest_flat, dest_flat, x, gate, g, b, ys)


def _moe_layer(x, router_w, router_b, experts, layer, g, b, tiles):
    t, d = x.shape
    tm, tt, tr = tiles["tm"], tiles["tt"], tiles["tr"]
    wpad = jnp.pad(router_w, ((0, 0), (0, LANES - N_EXPERTS)))
    wh = wpad.astype(BF16)
    wl = (wpad - wh.astype(F32)).astype(BF16)
    rb = jnp.pad(router_b, (0, LANES - N_EXPERTS))[None, :]
    e, gate, rank, cnt = _router(x, wh, wl, rb, tr)

    counts = cnt[0, :N_EXPERTS].astype(I32)
    padded = (counts + tm - 1) // tm * tm
    pend = jnp.cumsum(padded)
    start = jnp.pad(pend - padded, (0, LANES - N_EXPERTS))[None, :].astype(F32)
    ntile = t * TOP_K // tm + N_EXPERTS
    tile_row = jnp.arange(ntile, dtype=I32) * tm
    blk_e = jnp.minimum(jnp.sum(pend[None, :] <= tile_row[:, None], axis=1), N_EXPERTS - 1).astype(I32)

    dest = _dest(e, rank, start, tr).reshape(-1)
    meta = jnp.concatenate([counts, pend - padded, pend[-1:]]).astype(I32)
    xs = _dispatch(dest, meta, x, ntile * tm, tt, tm)
    ys = _expert_ffn(blk_e, xs, *experts, layer, tm)
    return _combine(dest, x, gate, g[None, :], b[None, :], ys, tt)


def _block_diag(w):
    n, c, _ = w.shape
    eye = jnp.eye(n, dtype=w.dtype)
    return (eye[:, None, :, None] * w[:, :, None, :]).reshape(n * c, n * c)


def kernel(x, mem, rel_bias, a_w_in, a_conv_w, a_conv_b, a_wr, a_br, a_wi, a_bi, a_lambda, b_w_in, b_kv_norm_g, b_w_uk, b_w_uv, b_idx_norm_g, b_idx_norm_b, w_mem_kv, w_out, ln1_g, ln1_b, router_w, router_b, exp_w1, exp_b1, exp_w2, exp_b2, ln2_g, ln2_b):
    bsz, seq, d = x.shape
    tiles = _tiles(seq)
    ts = tiles["ts"]
    row = lambda v: v[None, :]

    def finish_layer(layer, x, tok, mq):
        kv = _matmul(mem.reshape(-1, d), w_mem_kv[layer].astype(BF16), mem.shape[1])
        kv = kv.astype(BF16).reshape(bsz, mem.shape[1], -1)
        x1 = _post_mixer(x, tok, mq, kv, w_out[layer].astype(BF16), row(ln1_g[layer]),
                         row(ln1_b[layer]), ts)
        x2 = _moe_layer(x1.reshape(-1, d), router_w[layer], router_b[layer], experts, layer,
                        ln2_g[layer], ln2_b[layer], tiles)
        return x2.reshape(bsz, seq, d)

    experts = (exp_w1.astype(BF16), exp_b1[:, :, None, :], exp_w2.astype(BF16),
               exp_b2[:, :, None, :])

    tok, mq = _rglru_front(x, a_w_in[0].astype(BF16), a_conv_w[0], row(a_conv_b[0]),
                           _block_diag(a_wr[0]).astype(BF16), row(a_br[0]),
                           _block_diag(a_wi[0]).astype(BF16), row(a_bi[0]), row(a_lambda[0]), ts)
    x = finish_layer(0, x, tok, mq)

    nh = b_w_uk.shape[2]
    tok_w = nh * HEAD_DIM
    q0, c0, iq0, mq0, ik0, iw0, total = _dsa_layout(tok_w)
    w = b_w_in[0]
    o_c, o_iq = tok_w, tok_w + KV_LATENT
    o_ik = o_iq + IDX_HEADS * IDX_DIM
    o_iw = o_ik + IDX_DIM
    o_mq = o_iw + IDX_HEADS
    w_re = jnp.zeros((d, total), F32)
    w_re = w_re.at[:, q0:q0 + tok_w].set(w[:, :tok_w])
    w_re = w_re.at[:, c0:c0 + KV_LATENT].set(w[:, o_c:o_iq])
    w_re = w_re.at[:, iq0:iq0 + IDX_HEADS * IDX_DIM].set(w[:, o_iq:o_ik])
    w_re = w_re.at[:, ik0:ik0 + IDX_DIM].set(w[:, o_ik:o_iw])
    w_re = w_re.at[:, iw0:iw0 + IDX_HEADS].set(w[:, o_iw:o_mq])
    w_re = w_re.at[:, mq0:mq0 + MEM_WIDTH].set(w[:, o_mq:])
    wuk = jnp.transpose(b_w_uk[0], (1, 2, 0)).astype(BF16)
    wuv = jnp.transpose(b_w_uv[0], (1, 0, 2))
    wuv2 = jnp.zeros((nh // 2, 2 * KV_LATENT, 2 * HEAD_DIM), F32)
    wuv2 = wuv2.at[:, :KV_LATENT, :HEAD_DIM].set(wuv[0::2])
    wuv2 = wuv2.at[:, KV_LATENT:, HEAD_DIM:].set(wuv[1::2]).astype(BF16)
    qlat, c, iq, ik, iw, mq = _dsa_proj(x, w_re.astype(BF16), wuk, row(b_kv_norm_g[0]),
                                        row(b_idx_norm_g[0]), row(b_idx_norm_b[0]), ts)
    k_sel = min(TOPK_MAX, seq // 4)
    chunk = tiles["chunk"]
    ikt = jnp.swapaxes(ik.reshape(bsz, MASK_BITS, chunk, IDX_DIM), 2, 3)
    mask = _dsa_select(iq, iw, ikt, chunk, k_sel)
    tok = _dsa_attend(qlat, c, mask, _bias_tiles(rel_bias, tiles["tq_att"], chunk), wuv2,
                      tiles["tq_att"], chunk)
    x = finish_layer(1, x, tok, mq)
    return x
```

```python
import functools
import math

import jax
import jax.numpy as jnp
import numpy as np
from jax import lax
from jax.experimental import pallas as pl
from jax.experimental.pallas import tpu as pltpu

F32 = jnp.float32
BF16 = jnp.bfloat16
I32 = jnp.int32

HEAD_DIM = 64
MEM_HEADS = 4
MEM_WIDTH = MEM_HEADS * HEAD_DIM
LRU_C = 8.0
CONV_W = 4
KV_LATENT = 128
IDX_HEADS = 4
IDX_DIM = 64
TOPK_MAX = 256
REL_BUCKETS = 32
REL_MAX_DIST = 128
N_EXPERTS = 32
TOP_K = 4
SWIGLU_LIMIT = 7.0
SWIGLU_ALPHA = 1.702
DEPTH = 2
DN_ALPHA = (2 * DEPTH) ** 0.25
LN_EPS = 1e-5
RMS_EPS = 1e-6

SUBLANES = 8
LANES = 128
MASK_BITS = 32
INT_MIN = -2 ** 31
NEG_BIG = -1e30
VMEM_LIMIT = 56 * 1024 * 1024


def _tiles(seq):
    chunk = seq // MASK_BITS
    return dict(
        ts=min(512, seq),
        chunk=chunk,
        tq_att=2 * chunk,
        tt=512,
        tm=512,
        tr=512,
    )


def _dot(a, b):
    return jnp.dot(a, b, preferred_element_type=F32)


def _dot_nt(a, b):
    return lax.dot_general(a, b, (((1,), (1,)), ((), ())), preferred_element_type=F32)


def _sigmoid(x):
    return 1.0 / (1.0 + jnp.exp(-x))


def _layer_norm(z, g, b):
    mu = jnp.mean(z, axis=-1, keepdims=True)
    zc = z - mu
    var = jnp.mean(zc * zc, axis=-1, keepdims=True)
    return zc * lax.rsqrt(var + LN_EPS) * g + b


def _params(*sem):
    return pltpu.CompilerParams(dimension_semantics=sem, vmem_limit_bytes=VMEM_LIMIT)


def _mm_kernel(a_ref, b_ref, o_ref):
    o_ref[...] = _dot(a_ref[...].astype(BF16), b_ref[...])


def _matmul(a, b, tm):
    m, k = a.shape
    n = b.shape[1]
    return pl.pallas_call(
        _mm_kernel,
        grid=(m // tm,),
        in_specs=[pl.BlockSpec((tm, k), lambda i: (i, 0)),
                  pl.BlockSpec((k, n), lambda i: (0, 0))],
        out_specs=pl.BlockSpec((tm, n), lambda i: (i, 0)),
        out_shape=jax.ShapeDtypeStruct((m, n), F32),
        compiler_params=_params("arbitrary"),
        name="matmul",
    )(a, b)


def _rglru_kernel(x_ref, win_ref, cw_ref, cb_ref, wr_ref, br_ref, wi_ref, bi_ref, lam_ref,
                  tok_ref, mq_ref, tail_ref, h_ref, a_s, u_s, *, tok_w):
    s = pl.program_id(1)

    @pl.when(s == 0)
    def _():
        tail_ref[...] = jnp.zeros_like(tail_ref)
        h_ref[...] = jnp.zeros_like(h_ref)

    proj = _dot(x_ref[0].astype(BF16), win_ref[...])
    xb = proj[:, :tok_w]
    gb = proj[:, tok_w:2 * tok_w]
    mq_ref[0] = proj[:, 2 * tok_w:]
    ts = xb.shape[0]

    xe = jnp.concatenate([tail_ref[...], xb], axis=0)
    tail_ref[...] = xb[ts - SUBLANES:, :]
    xc = xb * cw_ref[CONV_W - 1:CONV_W, :] + cb_ref[...]
    for d in range(1, CONV_W):
        sh = pltpu.roll(xe, d, 0)[SUBLANES:, :]
        xc = xc + sh * cw_ref[CONV_W - 1 - d:CONV_W - d, :]

    xcb = xc.astype(BF16)
    r = _sigmoid(_dot(xcb, wr_ref[...]) + br_ref[...])
    ig = _sigmoid(_dot(xcb, wi_ref[...]) + bi_ref[...])
    nl = -lam_ref[...]
    softplus = jnp.maximum(nl, 0.0) + jnp.log1p(jnp.exp(-jnp.abs(nl)))
    log_a = (-LRU_C) * r * softplus
    a = jnp.exp(log_a)
    th = jnp.tanh(log_a)
    u = jnp.sqrt(-2.0 * th / (1.0 - th)) * (ig * xc)

    row = lax.broadcasted_iota(I32, a.shape, 0) & (SUBLANES - 1)
    for d in (1, 2, 4):
        m = row >= d
        u = jnp.where(m, a * pltpu.roll(u, d, 0) + u, u)
        a = jnp.where(m, a * pltpu.roll(a, d, 0), a)
    a_s[...] = a
    u_s[...] = u

    def group(g, h):
        sl = pl.ds(pl.multiple_of(g * SUBLANES, SUBLANES), SUBLANES)
        hg = u_s[sl, :] + a_s[sl, :] * h
        u_s[sl, :] = hg
        return jnp.broadcast_to(hg[SUBLANES - 1:, :], hg.shape)

    h_ref[...] = lax.fori_loop(0, ts // SUBLANES, group, h_ref[...])

    gelu = 0.5 * gb * (1.0 + jnp.tanh(math.sqrt(2.0 / math.pi) * (gb + 0.044715 * (gb * gb * gb))))
    tok_ref[0] = u_s[...] * gelu


def _rglru_front(x, w_in, conv_w, conv_b, wr, br, wi, bi, lam, ts):
    b, s, d = x.shape
    tok_w = conv_w.shape[-1]
    mem_w = w_in.shape[-1] - 2 * tok_w
    const = lambda shape: pl.BlockSpec(shape, lambda i, j: (0,) * len(shape))
    return pl.pallas_call(
        functools.partial(_rglru_kernel, tok_w=tok_w),
        grid=(b, s // ts),
        in_specs=[pl.BlockSpec((1, ts, d), lambda i, j: (i, j, 0)),
                  const(w_in.shape), const(conv_w.shape), const(conv_b.shape),
                  const(wr.shape), const(br.shape), const(wi.shape), const(bi.shape),
                  const(lam.shape)],
        out_specs=[pl.BlockSpec((1, ts, tok_w), lambda i, j: (i, j, 0)),
                   pl.BlockSpec((1, ts, mem_w), lambda i, j: (i, j, 0))],
        out_shape=[jax.ShapeDtypeStruct((b, s, tok_w), F32),
                   jax.ShapeDtypeStruct((b, s, mem_w), F32)],
        scratch_shapes=[pltpu.VMEM((SUBLANES, tok_w), F32), pltpu.VMEM((SUBLANES, tok_w), F32),
                        pltpu.VMEM((ts, tok_w), F32), pltpu.VMEM((ts, tok_w), F32)],
        compiler_params=_params("arbitrary", "arbitrary"),
        name="rglru_front",
    )(x, w_in, conv_w, conv_b, wr, br, wi, bi, lam)


def _dsa_layout(tok_w):
    q0 = 0
    c0 = q0 + tok_w
    iq0 = c0 + KV_LATENT
    mq0 = iq0 + IDX_HEADS * IDX_DIM
    ik0 = mq0 + MEM_WIDTH
    iw0 = ik0 + LANES
    total = iw0 + LANES
    return q0, c0, iq0, mq0, ik0, iw0, total


def _dsa_proj_kernel(x_ref, w_ref, wuk_ref, kvg_ref, ig_ref, ib_ref,
                     qlat_ref, c_ref, iq_ref, ik_ref, iw_ref, mq_ref, *, tok_w):
    q0, c0, iq0, mq0, ik0, iw0, _ = _dsa_layout(tok_w)
    proj = _dot(x_ref[0].astype(BF16), w_ref[...])
    for h in range(tok_w // HEAD_DIM):
        qh = proj[:, q0 + h * HEAD_DIM:q0 + (h + 1) * HEAD_DIM].astype(BF16)
        ql = _dot(qh, wuk_ref[h]) * HEAD_DIM ** -0.5
        qlat_ref[0, h] = ql.astype(BF16)
    c = proj[:, c0:c0 + KV_LATENT]
    c = c * lax.rsqrt(jnp.mean(c * c, axis=-1, keepdims=True) + RMS_EPS) * kvg_ref[...]
    c_ref[0] = c.astype(BF16)
    iq_ref[0] = proj[:, iq0:iq0 + IDX_HEADS * IDX_DIM].astype(BF16)
    ik = _layer_norm(proj[:, ik0:ik0 + IDX_DIM], ig_ref[...], ib_ref[...])
    ik_ref[0] = ik.astype(BF16)
    iw_ref[0] = proj[:, iw0:iw0 + IDX_HEADS] * (IDX_HEADS ** -0.5 * IDX_DIM ** -0.5)
    mq_ref[0] = proj[:, mq0:mq0 + MEM_WIDTH]


def _dsa_proj(x, w, wuk, kvg, ig, ib, ts):
    b, s, d = x.shape
    nh = wuk.shape[0]
    tok_w = nh * HEAD_DIM
    const = lambda shape: pl.BlockSpec(shape, lambda i, j: (0,) * len(shape))
    tile = lambda w_: pl.BlockSpec((1, ts, w_), lambda i, j: (i, j, 0))
    widths = [KV_LATENT, IDX_HEADS * IDX_DIM, IDX_DIM, IDX_HEADS, MEM_WIDTH]
    dtypes = [BF16, BF16, BF16, F32, F32]
    return pl.pallas_call(
        functools.partial(_dsa_proj_kernel, tok_w=tok_w),
        grid=(b, s // ts),
        in_specs=[tile(d), const(w.shape), const(wuk.shape), const(kvg.shape),
                  const(ig.shape), const(ib.shape)],
        out_specs=[pl.BlockSpec((1, nh, ts, KV_LATENT), lambda i, j: (i, 0, j, 0))]
        + [tile(w_) for w_ in widths],
        out_shape=[jax.ShapeDtypeStruct((b, nh, s, KV_LATENT), BF16)]
        + [jax.ShapeDtypeStruct((b, s, w_), t) for w_, t in zip(widths, dtypes)],
        compiler_params=_params("arbitrary", "arbitrary"),
        name="dsa_proj",
    )(x, w, wuk, kvg, ig, ib)


def _row_popcount(words):
    pc = lax.population_count(words)
    part = pc[:, :LANES]
    for i in range(1, words.shape[1] // LANES):
        part = part + pc[:, i * LANES:(i + 1) * LANES]
    return jnp.sum(part.astype(F32), axis=1, keepdims=True)


def _dsa_select_kernel(iq_ref, iw_ref, ikt_ref, mask_ref, keys_ref, iwb_ref, tie_ref,
                       *, k_sel, chunk):
    qi = pl.program_id(1)
    tq = iq_ref.shape[1]
    iq = iq_ref[0]
    iw = iw_ref[0]
    for h in range(IDX_HEADS):
        iwb_ref[h] = jnp.broadcast_to(iw[:, h:h + 1], (tq, chunk))
    iqh = [iq[:, h * IDX_DIM:(h + 1) * IDX_DIM] for h in range(IDX_HEADS)]

    def ukey(c):
        ikc = ikt_ref[0, c]
        sc = jnp.maximum(_dot(iqh[0], ikc), 0.0) * iwb_ref[0]
        for h in range(1, IDX_HEADS):
            sc = sc + jnp.maximum(_dot(iqh[h], ikc), 0.0) * iwb_ref[h]
        bits = lax.bitcast_convert_type(sc, I32)
        word = bits ^ ((bits >> 31) | INT_MIN)
        return word + (word == 0x7FFFFFFF).astype(I32)

    def score(c, _):
        keys_ref[c] = ukey(c)
        return 0

    lax.fori_loop(0, qi, score, 0)
    row = lax.broadcasted_iota(I32, (tq, chunk), 0)
    lane = lax.broadcasted_iota(I32, (tq, chunk), 1)
    keys_ref[qi] = jnp.where(lane <= row, ukey(qi), 0)

    def clear(c, _):
        keys_ref[c] = jnp.zeros((tq, chunk), I32)
        return 0

    lax.fori_loop(qi + 1, MASK_BITS, clear, 0)

    def transpose(g, _):
        rows = pl.ds(pl.multiple_of(g * SUBLANES, SUBLANES), SUBLANES)
        for half in range(chunk // LANES):
            cols = slice(half * LANES, (half + 1) * LANES)
            a = [keys_ref[c, rows, cols] for c in range(MASK_BITS)]
            j, m = MASK_BITS // 2, 0x0000FFFF
            while j:
                k = 0
                while k < MASK_BITS:
                    t = (lax.shift_right_logical(a[k], j) ^ a[k + j]) & m
                    a[k] = a[k] ^ (t << j)
                    a[k + j] = a[k + j] ^ t
                    k = (k + j + 1) & ~j
                j >>= 1
                m = (m ^ (m << j)) & 0xFFFFFFFF
                m = m - (1 << 32) if m >= (1 << 31) else m
            for c in range(MASK_BITS):
                keys_ref[c, rows, cols] = a[c]
        return 0

    lax.fori_loop(0, tq // SUBLANES, transpose, 0)

    def plane_step(i, carry):
        active, greater, need, any_one = carry
        ones = active & keys_ref[MASK_BITS - 1 - i]
        cnt = _row_popcount(ones)
        take = cnt >= need
        active = jnp.where(take, ones, active ^ ones)
        greater = jnp.where(take, greater, greater | ones)
        need = jnp.where(take, need, need - cnt)
        return active, greater, need, jnp.where(take, 1.0, any_one)

    full = jnp.full((tq, chunk), -1, I32)
    ties, greater, need, any_one = lax.fori_loop(
        0, MASK_BITS, plane_step,
        (full, jnp.zeros((tq, chunk), I32), jnp.full((tq, 1), float(k_sel), F32),
         jnp.zeros((tq, 1), F32)))
    ties = jnp.where(any_one > 0.0, ties, 0)
    tie_ref[...] = ties

    @pl.when(jnp.max(_row_popcount(ties) - need) > 0.0)
    def _():
        shift = chunk.bit_length() - 1
        lane_i = lax.broadcasted_iota(I32, (tq, chunk), 1)

        def below(v, inclusive):
            vh = jnp.broadcast_to(v >> shift, (tq, chunk))
            vl = jnp.broadcast_to(v & (chunk - 1), (tq, chunk))
            top = jnp.int32(1) << vh
            edge = (lane_i <= vl) if inclusive else (lane_i < vl)
            return (top - 1) | jnp.where(edge, top, 0)

        nbits = (MASK_BITS * chunk).bit_length() - 1

        def idx_step(i, j):
            cand = j | (jnp.int32(1) << (nbits - 1 - i))
            cnt = _row_popcount(ties & below(cand, False))
            return jnp.where(cnt < need, cand, j)

        last = lax.fori_loop(0, nbits, idx_step, jnp.zeros((tq, 1), I32))
        tie_ref[...] = ties & below(last, True)

    mask_ref[0] = greater | tie_ref[...]


def _dsa_select(iq, iw, ikt, chunk, k_sel):
    b, s, _ = iq.shape
    tq = chunk
    return pl.pallas_call(
        functools.partial(_dsa_select_kernel, k_sel=k_sel, chunk=chunk),
        grid=(b, s // tq),
        in_specs=[pl.BlockSpec((1, tq, iq.shape[2]), lambda i, j: (i, j, 0)),
                  pl.BlockSpec((1, tq, iw.shape[2]), lambda i, j: (i, j, 0)),
                  pl.BlockSpec((1,) + ikt.shape[1:], lambda i, j: (i, 0, 0, 0))],
        out_specs=pl.BlockSpec((1, tq, chunk), lambda i, j: (i, j, 0)),
        out_shape=jax.ShapeDtypeStruct((b, s, chunk), I32),
        scratch_shapes=[pltpu.VMEM((MASK_BITS, tq, chunk), I32),
                        pltpu.VMEM((IDX_HEADS, tq, chunk), F32),
                        pltpu.VMEM((tq, chunk), I32)],
        compiler_params=_params("arbitrary", "arbitrary"),
        name="dsa_select",
    )(iq, iw, ikt)


def _dsa_attend_kernel(qi_ref, kb_ref, qlat_ref, c_ref, mask_ref, bias_ref, wuv_ref, out_ref,
                       m_ref, acc_ref, madd_ref, *, nh, rg):
    p_id = pl.program_id(1)
    qi = qi_ref[p_id]
    kb = kb_ref[p_id]
    tq, tk = madd_ref.shape
    ngroups = tq // rg
    qmul = tq // tk
    first_near = qmul * qi - 1

    @pl.when(kb == 0)
    def _():
        m_ref[...] = jnp.full(m_ref.shape, NEG_BIG, F32)
        acc_ref[...] = jnp.zeros_like(acc_ref)

    madd_ref[...] = jnp.where(((mask_ref[0] >> kb) & 1) == 1, 0.0, NEG_BIG)
    c = c_ref[0]
    c_one = jnp.concatenate([c, jnp.ones_like(c)], axis=1)

    def heads(with_bias):
        for i in range(nh * ngroups):
            h, g = divmod(i, ngroups)
            rows = pl.ds(i * rg, rg)
            qrows = pl.ds(g * rg, rg)
            lg = _dot_nt(qlat_ref[0, h, qrows, :], c) + madd_ref[qrows, :]
            if with_bias:
                lg = lg + bias_ref[0, h, qrows, :]
            m_old = m_ref[rows, :]
            m_new = jnp.maximum(m_old, jnp.max(lg, axis=1, keepdims=True))
            alpha = jnp.exp(m_old - m_new)
            p = jnp.exp(lg - jnp.concatenate([m_new] * (tk // LANES), axis=1))
            m_ref[rows, :] = m_new
            acc_ref[rows, :] = (jnp.concatenate([alpha, alpha], axis=1) * acc_ref[rows, :]
                                + _dot(p.astype(BF16), c_one))

    @pl.when(kb >= first_near)
    def _():
        heads(True)

    @pl.when(kb < first_near)
    def _():
        heads(False)

    @pl.when(kb == first_near + qmul)
    def _():
        for j in range(nh // 2):
            r0, r1 = pl.ds(2 * j * tq, tq), pl.ds((2 * j + 1) * tq, tq)
            o = jnp.concatenate([acc_ref[r0, :KV_LATENT] / acc_ref[r0, KV_LATENT:],
                                 acc_ref[r1, :KV_LATENT] / acc_ref[r1, KV_LATENT:]], axis=1)
            out_ref[0, :, j * LANES:(j + 1) * LANES] = _dot(o.astype(BF16), wuv_ref[j])


def _dsa_attend(qlat, c, mask, bias, wuv2, tq, tk):
    b, nh, s, _ = qlat.shape
    nq = s // tq
    qmul = tq // tk
    rg = min(128, tq)
    qi_tab = np.concatenate([np.full(qmul * (q + 1), q, np.int32) for q in range(nq)])
    kb_tab = np.concatenate([np.arange(qmul * (q + 1), dtype=np.int32) for q in range(nq)])
    grid_spec = pltpu.PrefetchScalarGridSpec(
        num_scalar_prefetch=2,
        grid=(b, len(qi_tab)),
        in_specs=[pl.BlockSpec((1, nh, tq, KV_LATENT), lambda i, p, qt, kt: (i, 0, qt[p], 0)),
                  pl.BlockSpec((1, tk, c.shape[2]), lambda i, p, qt, kt: (i, kt[p], 0)),
                  pl.BlockSpec((1, tq, mask.shape[2]), lambda i, p, qt, kt: (i, qt[p], 0)),
                  pl.BlockSpec((1, nh, tq, tk),
                               lambda i, p, qt, kt: (jnp.clip(kt[p] - (qmul * qt[p] - 1), 0, qmul),
                                                     0, 0, 0)),
                  pl.BlockSpec(wuv2.shape, lambda i, p, qt, kt: (0, 0, 0))],
        out_specs=pl.BlockSpec((1, tq, nh * HEAD_DIM), lambda i, p, qt, kt: (i, qt[p], 0)),
        scratch_shapes=[pltpu.VMEM((nh * tq, LANES), F32),
                        pltpu.VMEM((nh * tq, 2 * KV_LATENT), F32),
                        pltpu.VMEM((tq, tk), F32)],
    )
    return pl.pallas_call(
        functools.partial(_dsa_attend_kernel, nh=nh, rg=rg),
        grid_spec=grid_spec,
        out_shape=jax.ShapeDtypeStruct((b, s, nh * HEAD_DIM), F32),
        compiler_params=_params("arbitrary", "arbitrary"),
        name="dsa_attend",
    )(jnp.asarray(qi_tab), jnp.asarray(kb_tab), qlat, c, mask, bias, wuv2)


def _t5_bucket(rel):
    n = jnp.maximum(rel, 0)
    max_exact = REL_BUCKETS // 2
    large = max_exact + (jnp.log(jnp.maximum(n, 1).astype(F32) / max_exact)
                         / math.log(REL_MAX_DIST / max_exact) * (REL_BUCKETS - max_exact)).astype(I32)
    large = jnp.minimum(large, REL_BUCKETS - 1)
    return jnp.where(n < max_exact, n, large)


def _bias_tiles(rel_bias, tq, tk):
    assert tk >= REL_MAX_DIST
    rel0 = jnp.arange(tq, dtype=I32)[:, None] - jnp.arange(tk, dtype=I32)[None, :]
    shifted = rel_bias - rel_bias[REL_BUCKETS - 1]
    buckets = jnp.stack([_t5_bucket(rel0 + (1 - j) * tk) for j in range(tq // tk + 1)])
    onehot = jax.nn.one_hot(buckets, REL_BUCKETS, dtype=F32)
    return jnp.einsum('dijb,bh->dhij', onehot, shifted, precision=lax.Precision.HIGHEST)


def _post_kernel(x_ref, tok_ref, mq_ref, kv_ref, wout_ref, g_ref, b_ref, x1_ref, *, tok_w):
    mq = mq_ref[0]
    kv = kv_ref[0]
    y = _dot(tok_ref[0].astype(BF16), wout_ref[:tok_w, :])
    for h in range(MEM_HEADS):
        q = mq[:, h * HEAD_DIM:(h + 1) * HEAD_DIM].astype(BF16)
        k = kv[:, h * HEAD_DIM:(h + 1) * HEAD_DIM]
        v = kv[:, MEM_WIDTH + h * HEAD_DIM:MEM_WIDTH + (h + 1) * HEAD_DIM]
        lg = _dot_nt(q, k) * HEAD_DIM ** -0.5
        p = jnp.exp(lg - jnp.max(lg, axis=1, keepdims=True)).astype(BF16)
        o = _dot(p, v) / _dot(p, jnp.ones((p.shape[1], LANES), BF16))[:, :HEAD_DIM]
        y = y + _dot(o.astype(BF16), wout_ref[tok_w + h * HEAD_DIM:tok_w + (h + 1) * HEAD_DIM, :])
    x1_ref[0] = _layer_norm(DN_ALPHA * x_ref[0] + y, g_ref[...], b_ref[...])


def _post_mixer(x, tok, mq, kv, w_out, g, b, ts):
    bsz, s, d = x.shape
    tok_w = tok.shape[2]
    const = lambda shape: pl.BlockSpec(shape, lambda i, j: (0,) * len(shape))
    tile = lambda w_: pl.BlockSpec((1, ts, w_), lambda i, j: (i, j, 0))
    return pl.pallas_call(
        functools.partial(_post_kernel, tok_w=tok_w),
        grid=(bsz, s // ts),
        in_specs=[tile(d), tile(tok_w), tile(mq.shape[2]),
                  pl.BlockSpec((1,) + kv.shape[1:], lambda i, j: (i, 0, 0)),
                  const(w_out.shape), const(g.shape), const(b.shape)],
        out_specs=tile(d),
        out_shape=jax.ShapeDtypeStruct((bsz, s, d), F32),
        compiler_params=_params("arbitrary", "arbitrary"),
        name="post_mixer",
    )(x, tok, mq, kv, w_out, g, b)


def _router_kernel(x_ref, wh_ref, wl_ref, rb_ref, e_ref, gate_ref, rank_ref, cnt_ref, carry_ref):
    @pl.when(pl.program_id(0) == 0)
    def _():
        carry_ref[...] = jnp.zeros_like(carry_ref)

    x = x_ref[...]
    tr = x.shape[0]
    xh = x.astype(BF16)
    xl = (x - xh.astype(F32)).astype(BF16)
    lg = _dot(xh, wh_ref[...]) + (_dot(xl, wh_ref[...]) + _dot(xh, wl_ref[...])) + rb_ref[...]
    lane = lax.broadcasted_iota(I32, lg.shape, 1).astype(F32)
    lg = jnp.where(lane < N_EXPERTS, lg, -jnp.inf)

    idx, val = [], []
    onehot = jnp.zeros(lg.shape, F32)
    for _ in range(TOP_K):
        m = jnp.max(lg, axis=1, keepdims=True)
        i = jnp.min(jnp.where(lg == m, lane, float(LANES)), axis=1, keepdims=True)
        hit = lane == i
        idx.append(i)
        val.append(m)
        onehot = jnp.where(hit, 1.0, onehot)
        lg = jnp.where(hit, -jnp.inf, lg)

    ex = [jnp.exp(v - val[0]) for v in val]
    den = ex[0] + ex[1] + ex[2] + ex[3]

    r_i = lax.broadcasted_iota(I32, (tr, tr), 0)
    c_i = lax.broadcasted_iota(I32, (tr, tr), 1)
    tri = jnp.where(c_i < r_i, 1.0, 0.0).astype(BF16)
    before = _dot(tri, onehot.astype(BF16)) + carry_ref[...]
    carry_ref[...] = carry_ref[...] + jnp.sum(onehot, axis=0, keepdims=True)
    cnt_ref[...] = carry_ref[...]

    lane_k = lax.broadcasted_iota(I32, (tr, TOP_K), 1)
    e_out = jnp.zeros((tr, TOP_K), I32)
    g_out = jnp.zeros((tr, TOP_K), F32)
    r_out = jnp.zeros((tr, TOP_K), I32)
    for k in range(TOP_K):
        rk = jnp.sum(jnp.where(lane == idx[k], before, 0.0), axis=1, keepdims=True).astype(I32)
        e_out = jnp.where(lane_k == k, idx[k].astype(I32), e_out)
        g_out = jnp.where(lane_k == k, ex[k] / den, g_out)
        r_out = jnp.where(lane_k == k, rk, r_out)
    e_ref[...] = e_out
    gate_ref[...] = g_out
    rank_ref[...] = r_out


def _router(x, wh, wl, rb, tr):
    t, d = x.shape
    const = lambda shape: pl.BlockSpec(shape, lambda i: (0,) * len(shape))
    small = pl.BlockSpec((tr, TOP_K), lambda i: (i, 0))
    return pl.pallas_call(
        _router_kernel,
        grid=(t // tr,),
        in_specs=[pl.BlockSpec((tr, d), lambda i: (i, 0)), const(wh.shape), const(wl.shape),
                  const(rb.shape)],
        out_specs=[small, small, small, const((1, LANES))],
        out_shape=[jax.ShapeDtypeStruct((t, TOP_K), I32), jax.ShapeDtypeStruct((t, TOP_K), F32),
                   jax.ShapeDtypeStruct((t, TOP_K), I32), jax.ShapeDtypeStruct((1, LANES), F32)],
        scratch_shapes=[pltpu.VMEM((1, LANES), F32)],
        compiler_params=_params("arbitrary"),
        name="router",
    )(x, wh, wl, rb)


def _dest_kernel(e_ref, rank_ref, start_ref, dest_ref):
    e = e_ref[...]
    lane = lax.broadcasted_iota(I32, (e.shape[0], LANES), 1)
    lane_k = lax.broadcasted_iota(I32, e.shape, 1)
    out = rank_ref[...]
    for k in range(TOP_K):
        st = jnp.sum(jnp.where(lane == e[:, k:k + 1], start_ref[...], 0.0), axis=1, keepdims=True)
        out = out + jnp.where(lane_k == k, st.astype(I32), 0)
    dest_ref[...] = out


def _dest(e, rank, start, tr):
    t = e.shape[0]
    small = pl.BlockSpec((tr, TOP_K), lambda i: (i, 0))
    return pl.pallas_call(
        _dest_kernel,
        grid=(t // tr,),
        in_specs=[small, small, pl.BlockSpec((1, LANES), lambda i: (0, 0))],
        out_specs=small,
        out_shape=jax.ShapeDtypeStruct((t, TOP_K), I32),
        compiler_params=_params("arbitrary"),
        name="dest",
    )(e, rank, start)


def _dispatch_kernel(dest_ref, meta_ref, x_ref, xs_ref, zero_ref, sem, *, tm):
    tt = x_ref.shape[0]
    ntail = (xs_ref.shape[0] - meta_ref[2 * N_EXPERTS]) // tm

    def pad_copy(e, r):
        row = meta_ref[N_EXPERTS + e] + r
        return pltpu.make_async_copy(zero_ref.at[pl.ds(0, 1)], xs_ref.at[pl.ds(row, 1)], sem)

    def tail_copy(j):
        row = pl.multiple_of(meta_ref[2 * N_EXPERTS] + j * tm, tm)
        return pltpu.make_async_copy(zero_ref, xs_ref.at[pl.ds(row, tm)], sem)

    def for_each_pad(fn):
        def per_expert(e, _):
            width = meta_ref[N_EXPERTS + e + 1] - meta_ref[N_EXPERTS + e]
            lax.fori_loop(meta_ref[e], width, lambda r, c: (fn(pad_copy(e, r)), c)[1], 0)
            return 0
        lax.fori_loop(0, N_EXPERTS, per_expert, 0)
        lax.fori_loop(0, ntail, lambda j, c: (fn(tail_copy(j)), c)[1], 0)

    @pl.when(pl.program_id(0) == 0)
    def _():
        zero_ref[...] = jnp.zeros_like(zero_ref)
        for_each_pad(lambda cp: cp.start())
        for_each_pad(lambda cp: cp.wait())

    def row_copy(i, k):
        d = dest_ref[i * TOP_K + k]
        return pltpu.make_async_copy(x_ref.at[pl.ds(i, 1)], xs_ref.at[pl.ds(d, 1)], sem)

    def issue(i, _):
        for k in range(TOP_K):
            row_copy(i, k).start()
        return 0

    def drain(i, _):
        for k in range(TOP_K):
            row_copy(i, k).wait()
        return 0

    lax.fori_loop(0, tt, issue, 0, unroll=8)
    lax.fori_loop(0, tt, drain, 0, unroll=8)


def _dispatch(dest_flat, meta, x, rows, tt, tm):
    t, d = x.shape
    return pl.pallas_call(
        functools.partial(_dispatch_kernel, tm=tm),
        grid=(t // tt,),
        in_specs=[pl.BlockSpec((tt * TOP_K,), lambda i: (i,), memory_space=pltpu.SMEM),
                  pl.BlockSpec(memory_space=pltpu.SMEM),
                  pl.BlockSpec((tt, d), lambda i: (i, 0))],
        out_specs=pl.BlockSpec(memory_space=pl.ANY),
        out_shape=jax.ShapeDtypeStruct((rows, d), x.dtype),
        scratch_shapes=[pltpu.VMEM((tm, d), x.dtype), pltpu.SemaphoreType.DMA(())],
        compiler_params=_params("arbitrary"),
        name="dispatch",
    )(dest_flat, meta, x)


def _ffn_kernel(be_ref, xs_ref, w1_ref, b1_ref, w2_ref, b2_ref, ys_ref):
    del be_ref
    de = w2_ref.shape[2]
    h = _dot(xs_ref[...].astype(BF16), w1_ref[0, 0]) + b1_ref[0, 0]
    gt = jnp.minimum(h[:, :de], SWIGLU_LIMIT)
    up = jnp.clip(h[:, de:], -SWIGLU_LIMIT, SWIGLU_LIMIT)
    act = (up + 1.0) * (gt * _sigmoid(SWIGLU_ALPHA * gt))
    ys_ref[...] = _dot(act.astype(BF16), w2_ref[0, 0]) + b2_ref[0, 0]


def _expert_ffn(blk_e, xs, w1, b1, w2, b2, layer, tm):
    rows, d = xs.shape
    expert = lambda a: pl.BlockSpec((1, 1) + a.shape[2:], lambda i, be: (layer, be[i], 0, 0))
    grid_spec = pltpu.PrefetchScalarGridSpec(
        num_scalar_prefetch=1,
        grid=(rows // tm,),
        in_specs=[pl.BlockSpec((tm, d), lambda i, be: (i, 0)),
                  expert(w1), expert(b1), expert(w2), expert(b2)],
        out_specs=pl.BlockSpec((tm, d), lambda i, be: (i, 0)),
    )
    return pl.pallas_call(
        _ffn_kernel,
        grid_spec=grid_spec,
        out_shape=jax.ShapeDtypeStruct((rows, d), F32),
        compiler_params=_params("arbitrary"),
        name="expert_ffn",
    )(blk_e, xs, w1, b1, w2, b2)


def _combine_kernel(dest_ref, dest_next_ref, x_ref, gate_ref, g_ref, b_ref, ys_ref, out_ref,
                    buf, sem):
    tt = x_ref.shape[0]
    step = pl.program_id(0)
    slot = step % 2

    def row_copy(idx_ref, to, i, k):
        d = idx_ref[i * TOP_K + k]
        return pltpu.make_async_copy(ys_ref.at[pl.ds(d, 1)], buf.at[to, k, pl.ds(i, 1)],
                                     sem.at[to])

    def issue(idx_ref, to):
        def body(i, _):
            for k in range(TOP_K):
                row_copy(idx_ref, to, i, k).start()
            return 0
        lax.fori_loop(0, tt, body, 0, unroll=8)

    @pl.when(step == 0)
    def _():
        issue(dest_ref, slot)

    @pl.when(step + 1 < pl.num_programs(0))
    def _():
        issue(dest_next_ref, 1 - slot)

    def drain(i, _):
        for k in range(TOP_K):
            row_copy(dest_ref, slot, i, k).wait()
        return 0

    lax.fori_loop(0, tt, drain, 0, unroll=8)

    gate = gate_ref[...]
    ffn = gate[:, 0:1] * buf[slot, 0]
    for k in range(1, TOP_K):
        ffn = ffn + gate[:, k:k + 1] * buf[slot, k]
    out_ref[...] = _layer_norm(DN_ALPHA * x_ref[...] + ffn, g_ref[...], b_ref[...])


def _combine(dest_flat, x, gate, g, b, ys, tt):
    t, d = x.shape
    const = lambda shape: pl.BlockSpec(shape, lambda i: (0,) * len(shape))
    nstep = t // tt
    return pl.pallas_call(
        _combine_kernel,
        grid=(nstep,),
        in_specs=[pl.BlockSpec((tt * TOP_K,), lambda i: (i,), memory_space=pltpu.SMEM),
                  pl.BlockSpec((tt * TOP_K,), lambda i: (jnp.minimum(i + 1, nstep - 1),),
                               memory_space=pltpu.SMEM),
                  pl.BlockSpec((tt, d), lambda i: (i, 0)),
                  pl.BlockSpec((tt, TOP_K), lambda i: (i, 0)),
                  const(g.shape), const(b.shape),
                  pl.BlockSpec(memory_space=pl.ANY)],
        out_specs=pl.BlockSpec((tt, d), lambda i: (i, 0)),
        out_shape=jax.ShapeDtypeStruct((t, d), F32),
        scratch_shapes=[pltpu.VMEM((2, TOP_K, tt, d), F32), pltpu.SemaphoreType.DMA((2,))],
        compiler_params=_params("arbitrary"),
        name="combine",
    )(dest_flat, dest_flat, x, gate, g, b, ys)


def _moe_layer(x, router_w, router_b, experts, layer, g, b, tiles):
    t, d = x.shape
    tm, tt, tr = tiles["tm"], tiles["tt"], tiles["tr"]
    wpad = jnp.pad(router_w, ((0, 0), (0, LANES - N_EXPERTS)))
    wh = wpad.astype(BF16)
    wl = (wpad - wh.astype(F32)).astype(BF16)
    rb = jnp.pad(router_b, (0, LANES - N_EXPERTS))[None, :]
    e, gate, rank, cnt = _router(x, wh, wl, rb, tr)

    counts = cnt[0, :N_EXPERTS].astype(I32)
    padded = (counts + tm - 1) // tm * tm
    pend = jnp.cumsum(padded)
    start = jnp.pad(pend - padded, (0, LANES - N_EXPERTS))[None, :].astype(F32)
    ntile = t * TOP_K // tm + N_EXPERTS
    tile_row = jnp.arange(ntile, dtype=I32) * tm
    blk_e = jnp.minimum(jnp.sum(pend[None, :] <= tile_row[:, None], axis=1), N_EXPERTS - 1).astype(I32)

    dest = _dest(e, rank, start, tr).reshape(-1)
    meta = jnp.concatenate([counts, pend - padded, pend[-1:]]).astype(I32)
    xs = _dispatch(dest, meta, x, ntile * tm, tt, tm)
    ys = _expert_ffn(blk_e, xs, *experts, layer, tm)
    return _combine(dest, x, gate, g[None, :], b[None, :], ys, tt)


def _block_diag(w):
    n, c, _ = w.shape
    eye = jnp.eye(n, dtype=w.dtype)
    return (eye[:, None, :, None] * w[:, :, None, :]).reshape(n * c, n * c)


def kernel(x, mem, rel_bias, a_w_in, a_conv_w, a_conv_b, a_wr, a_br, a_wi, a_bi, a_lambda, b_w_in, b_kv_norm_g, b_w_uk, b_w_uv, b_idx_norm_g, b_idx_norm_b, w_mem_kv, w_out, ln1_g, ln1_b, router_w, router_b, exp_w1, exp_b1, exp_w2, exp_b2, ln2_g, ln2_b):
    bsz, seq, d = x.shape
    tiles = _tiles(seq)
    ts = tiles["ts"]
    row = lambda v: v[None, :]

    def finish_layer(layer, x, tok, mq):
        kv = _matmul(mem.reshape(-1, d), w_mem_kv[layer].astype(BF16), mem.shape[1])
        kv = kv.astype(BF16).reshape(bsz, mem.shape[1], -1)
        x1 = _post_mixer(x, tok, mq, kv, w_out[layer].astype(BF16), row(ln1_g[layer]),
                         row(ln1_b[layer]), ts)
        x2 = _moe_layer(x1.reshape(-1, d), router_w[layer], router_b[layer], experts, layer,
                        ln2_g[layer], ln2_b[layer], tiles)
        return x2.reshape(bsz, seq, d)

    experts = (exp_w1.astype(BF16), exp_b1[:, :, None, :], exp_w2.astype(BF16),
               exp_b2[:, :, None, :])

    tok, mq = _rglru_front(x, a_w_in[0].astype(BF16), a_conv_w[0], row(a_conv_b[0]),
                           _block_diag(a_wr[0]).astype(BF16), row(a_br[0]),
                           _block_diag(a_wi[0]).astype(BF16), row(a_bi[0]), row(a_lambda[0]), ts)
    x = finish_layer(0, x, tok, mq)

    nh = b_w_uk.shape[2]
    tok_w = nh * HEAD_DIM
    q0, c0, iq0, mq0, ik0, iw0, total = _dsa_layout(tok_w)
    w = b_w_in[0]
    o_c, o_iq = tok_w, tok_w + KV_LATENT
    o_ik = o_iq + IDX_HEADS * IDX_DIM
    o_iw = o_ik + IDX_DIM
    o_mq = o_iw + IDX_HEADS
    w_re = jnp.zeros((d, total), F32)
    w_re = w_re.at[:, q0:q0 + tok_w].set(w[:, :tok_w])
    w_re = w_re.at[:, c0:c0 + KV_LATENT].set(w[:, o_c:o_iq])
    w_re = w_re.at[:, iq0:iq0 + IDX_HEADS * IDX_DIM].set(w[:, o_iq:o_ik])
    w_re = w_re.at[:, ik0:ik0 + IDX_DIM].set(w[:, o_ik:o_iw])
    w_re = w_re.at[:, iw0:iw0 + IDX_HEADS].set(w[:, o_iw:o_mq])
    w_re = w_re.at[:, mq0:mq0 + MEM_WIDTH].set(w[:, o_mq:])
    wuk = jnp.transpose(b_w_uk[0], (1, 2, 0)).astype(BF16)
    wuv = jnp.transpose(b_w_uv[0], (1, 0, 2))
    wuv2 = jnp.zeros((nh // 2, 2 * KV_LATENT, 2 * HEAD_DIM), F32)
    wuv2 = wuv2.at[:, :KV_LATENT, :HEAD_DIM].set(wuv[0::2])
    wuv2 = wuv2.at[:, KV_LATENT:, HEAD_DIM:].set(wuv[1::2]).astype(BF16)
    qlat, c, iq, ik, iw, mq = _dsa_proj(x, w_re.astype(BF16), wuk, row(b_kv_norm_g[0]),
                                        row(b_idx_norm_g[0]), row(b_idx_norm_b[0]), ts)
    k_sel = min(TOPK_MAX, seq // 4)
    chunk = tiles["chunk"]
    ikt = jnp.swapaxes(ik.reshape(bsz, MASK_BITS, chunk, IDX_DIM), 2, 3)
    mask = _dsa_select(iq, iw, ikt, chunk, k_sel)
    tok = _dsa_attend(qlat, c, mask, _bias_tiles(rel_bias, tiles["tq_att"], chunk), wuv2,
                      tiles["tq_att"], chunk)
    x = finish_layer(1, x, tok, mq)
    return x
```
